```python
import jax, jax.numpy as jnp
from jax import lax
import numpy as np

D_MODEL = 2048
BATCH = 2
SEQ = 8192
DEPTH = 1

HEAD_DIM = 128
ATTN_WIDTH = D_MODEL // 2
N_ATTN_HEADS = ATTN_WIDTH // HEAD_DIM
POOL_WIDTH = D_MODEL - ATTN_WIDTH
POOL_SIZES = (2, 4, 8, 16)
N_POOL_GROUPS = len(POOL_SIZES)
POOL_GROUP = POOL_WIDTH // N_POOL_GROUPS
MIX_WIDTH = ATTN_WIDTH + POOL_WIDTH
IN_WIDTH = 3 * ATTN_WIDTH + POOL_WIDTH
DILATED_PATTERNS = ((128, 1), (512, 4), (2048, 16))
ATTN_BLOCK = 128
ROPE_THETA = 500000.0
ROPE_DIM = HEAD_DIM // 4
PEER_HEADS = 8
PEER_NKEYS = 128
PEER_EXPERTS = PEER_NKEYS * PEER_NKEYS
PEER_QDIM = 256
PEER_TOPK = 16
PEER_CHUNK = 128
NORM_EPS = 1e-6
NEG_BIG = -1e30

kernel_name = "hybrid_dilated_pool_peer_adaln"


def _rmsnorm(x, g):
    xf = x.astype(jnp.float32)
    y = xf * lax.rsqrt(jnp.mean(xf * xf, axis=-1, keepdims=True) + NORM_EPS)
    return (y * g.astype(jnp.float32)).astype(x.dtype)


def _modulate(h, shift, scale):
    return h * (1 + scale[:, None, :]) + shift[:, None, :]


def _rotary(x, positions):
    half = ROPE_DIM // 2
    inv = ROPE_THETA ** (-jnp.arange(half, dtype=jnp.float32) * 2.0 / ROPE_DIM)
    ang = positions.astype(jnp.float32)[..., None] * inv
    cos = jnp.cos(ang)[:, :, None, :]
    sin = jnp.sin(ang)[:, :, None, :]
    xf = x.astype(jnp.float32)
    x1 = xf[..., :half]
    x2 = xf[..., half:ROPE_DIM]
    out = jnp.concatenate([x1 * cos - x2 * sin, x2 * cos + x1 * sin, xf[..., ROPE_DIM:]], axis=-1)
    return out.astype(x.dtype)


def _dilated_window_attention(q, k, v, window, dil):
    B, H, S, hd = q.shape
    n_back = window // dil
    L = S // dil
    nb = -(-L // ATTN_BLOCK)
    Lp = nb * ATTN_BLOCK

    def to_blocks(a):
        a = a.reshape(B, H, L, dil, hd).transpose(0, 1, 3, 2, 4)
        a = jnp.pad(a, ((0, 0), (0, 0), (0, 0), (0, Lp - L), (0, 0)))
        return a.reshape(B, H, dil, nb, ATTN_BLOCK, hd)

    def with_prev(a):
        prev = jnp.pad(a, ((0, 0), (0, 0), (0, 0), (1, 0), (0, 0), (0, 0)))[:, :, :, :nb]
        return jnp.concatenate([prev, a], axis=4)

    qb, kb, vb = to_blocks(q), to_blocks(k), to_blocks(v)
    kk, vv = with_prev(kb), with_prev(vb)
    s = jnp.einsum('bhrnqd,bhrnkd->bhrnqk', qb, kk, preferred_element_type=jnp.float32)
    q_idx = jnp.arange(nb)[:, None, None] * ATTN_BLOCK + jnp.arange(ATTN_BLOCK)[None, :, None]
    k_idx = jnp.arange(nb)[:, None, None] * ATTN_BLOCK - ATTN_BLOCK + jnp.arange(2 * ATTN_BLOCK)[None, None, :]
    dist = q_idx - k_idx
    valid = (dist >= 0) & (dist <= n_back) & (k_idx >= 0)
    s = jnp.where(valid, s, NEG_BIG)
    m = jnp.max(s, axis=-1, keepdims=True)
    p = jnp.exp(s - m)
    den = jnp.sum(p, axis=-1, keepdims=True)
    o = jnp.einsum('bhrnqk,bhrnkd->bhrnqd', p, vv.astype(jnp.float32)) / den
    lse = (m + jnp.log(den))[..., 0]

    def from_blocks(a, tail):
        a = a.reshape((B, H, dil, Lp) + tail)[:, :, :, :L]
        a = jnp.moveaxis(a, 2, 3)
        return a.reshape((B, H, S) + tail)

    return from_blocks(o, (hd,)), from_blocks(lse, ())


def _mixed_dilated_attention(q, k, v):
    outs, lses = [], []
    for window, dil in DILATED_PATTERNS:
        o, l = _dilated_window_attention(q, k, v, window, dil)
        outs.append(o)
        lses.append(l)
    w = jax.nn.softmax(jnp.stack(lses, axis=0), axis=0)
    return jnp.einsum('pbhs,pbhsd->bhsd', w, jnp.stack(outs, axis=0))


def _multiscale_pool(u, w_pool, pool_scale):
    B, S, _ = u.shape
    uf = u.astype(jnp.float32).reshape(B, S, N_POOL_GROUPS, POOL_GROUP)
    cs = jnp.cumsum(uf, axis=1)
    t = jnp.arange(S)
    means = []
    for gi, p in enumerate(POOL_SIZES):
        c_g = cs[:, :, gi]
        prev = jnp.pad(c_g, ((0, 0), (p, 0), (0, 0)))[:, :S]
        cnt = jnp.minimum(t + 1, p).astype(jnp.float32)[None, :, None]
        means.append((c_g - prev) / cnt)
    r = jnp.stack(means, axis=2) - uf
    y = jnp.einsum('bsgc,gce->bsge', r, w_pool.astype(jnp.float32))
    return (y.reshape(B, S, POOL_WIDTH) * pool_scale.astype(jnp.float32)).astype(u.dtype)


def _peer(h, w_query, sub_keys, peer_u, peer_v):
    B, S, D = h.shape
    T = B * S
    xt = h.reshape(T, D)
    q = (xt @ w_query).reshape(T, PEER_HEADS, 2, PEER_QDIM // 2).astype(jnp.float32)
    scores = jnp.einsum('thpc,hpnc->thpn', q, sub_keys.astype(jnp.float32))
    s_half, i_half = lax.top_k(scores, PEER_TOPK)
    cand = (s_half[:, :, 0, :, None] + s_half[:, :, 1, None, :]).reshape(T, PEER_HEADS, PEER_TOPK * PEER_TOPK)
    cand_idx = (i_half[:, :, 0, :, None] * PEER_NKEYS + i_half[:, :, 1, None, :]).reshape(T, PEER_HEADS, PEER_TOPK * PEER_TOPK)
    top_s, pos = lax.top_k(cand, PEER_TOPK)
    idx = jnp.take_along_axis(cand_idx, pos, axis=-1)
    gate = jax.nn.softmax(top_s, axis=-1)
    n_chunks = T // PEER_CHUNK

    def chunk(args):
        xc, ic, gc = args
        u = peer_u[ic]
        a = jnp.einsum('cd,chkd->chk', xc, u, preferred_element_type=jnp.float32)
        act = jax.nn.gelu(a, approximate=False) * gc
        vg = peer_v[ic]
        return jnp.einsum('chk,chkd->cd', act, vg.astype(jnp.float32)).astype(h.dtype)

    y = lax.map(chunk, (xt.reshape(n_chunks, PEER_CHUNK, D),
                        idx.reshape(n_chunks, PEER_CHUNK, PEER_HEADS, PEER_TOPK),
                        gate.reshape(n_chunks, PEER_CHUNK, PEER_HEADS, PEER_TOPK)))
    return y.reshape(B, S, D)


def setup_inputs(seed: int = 0) -> dict:
    key = jax.random.key(seed)
    ks = jax.random.split(key, 16)
    f32 = jnp.float32
    x = jax.random.normal(ks[0], (BATCH, SEQ, D_MODEL), f32)
    c = jax.random.normal(ks[1], (BATCH, D_MODEL), f32)
    positions = (jnp.arange(SEQ, dtype=jnp.int32)[None, :]
                 + jax.random.randint(ks[2], (BATCH, 1), 0, 1024, dtype=jnp.int32))
    w_mod = jax.random.normal(ks[3], (DEPTH, D_MODEL, 6 * D_MODEL), f32) * D_MODEL ** -0.5
    b_mod = jax.random.normal(ks[4], (DEPTH, 6 * D_MODEL), f32) * 0.02
    norm1_g = 1.0 + 0.02 * jax.random.normal(ks[5], (DEPTH, D_MODEL), f32)
    w_in = jax.random.normal(ks[6], (DEPTH, D_MODEL, IN_WIDTH), f32) * D_MODEL ** -0.5
    w_pool = jax.random.normal(ks[7], (DEPTH, N_POOL_GROUPS, POOL_GROUP, POOL_GROUP), f32) * POOL_GROUP ** -0.5
    pool_scale = 1.0 + 0.1 * jax.random.normal(ks[8], (DEPTH, POOL_WIDTH), f32)
    w_out = jax.random.normal(ks[9], (DEPTH, MIX_WIDTH, D_MODEL), f32) * MIX_WIDTH ** -0.5
    norm2_g = 1.0 + 0.02 * jax.random.normal(ks[10], (DEPTH, D_MODEL), f32)
    w_query = jax.random.normal(ks[11], (DEPTH, D_MODEL, PEER_HEADS * PEER_QDIM), f32) * D_MODEL ** -0.5
    sub_keys = jax.random.normal(ks[12], (DEPTH, PEER_HEADS, 2, PEER_NKEYS, PEER_QDIM // 2), f32) * (PEER_QDIM // 2) ** -0.5
    peer_u = jax.random.normal(ks[13], (DEPTH, PEER_EXPERTS, D_MODEL), f32) * D_MODEL ** -0.5
    peer_v = jax.random.normal(ks[14], (DEPTH, PEER_EXPERTS, D_MODEL), f32) * PEER_HEADS ** -0.5
    final_g = 1.0 + 0.02 * jax.random.normal(ks[15], (D_MODEL,), f32)
    return {"x": x, "c": c, "positions": positions, "w_mod": w_mod, "b_mod": b_mod,
            "norm1_g": norm1_g, "w_in": w_in, "w_pool": w_pool, "pool_scale": pool_scale,
            "w_out": w_out, "norm2_g": norm2_g, "w_query": w_query, "sub_keys": sub_keys,
            "peer_u": peer_u, "peer_v": peer_v, "final_g": final_g}


def reference(x, c, positions, w_mod, b_mod, norm1_g, w_in, w_pool, pool_scale,
              w_out, norm2_g, w_query, sub_keys, peer_u, peer_v, final_g):
    B, S, D = x.shape
    A = ATTN_WIDTH
    for l in range(DEPTH):
        mod = jax.nn.silu(c) @ w_mod[l] + b_mod[l]
        shift1, scale1, gate1, shift2, scale2, gate2 = jnp.split(mod, 6, axis=-1)

        h = _modulate(_rmsnorm(x, norm1_g[l]), shift1, scale1)
        z = h @ w_in[l]
        q, k, v, u = jnp.split(z, [A, 2 * A, 3 * A], axis=-1)
        q = _rotary(q.reshape(B, S, N_ATTN_HEADS, HEAD_DIM), positions) * (HEAD_DIM ** -0.5)
        k = _rotary(k.reshape(B, S, N_ATTN_HEADS, HEAD_DIM), positions)
        v = v.reshape(B, S, N_ATTN_HEADS, HEAD_DIM)
        q, k, v = (t.transpose(0, 2, 1, 3) for t in (q, k, v))
        attn = _mixed_dilated_attention(q, k, v)
        attn = attn.transpose(0, 2, 1, 3).reshape(B, S, A).astype(x.dtype)
        pool = _multiscale_pool(u, w_pool[l], pool_scale[l])
        mix = jnp.concatenate([attn, pool], axis=-1) @ w_out[l]
        x = x + gate1[:, None, :] * mix

        h = _modulate(_rmsnorm(x, norm2_g[l]), shift2, scale2)
        x = x + gate2[:, None, :] * _peer(h, w_query[l], sub_keys[l], peer_u[l], peer_v[l])
    return _rmsnorm(x, final_g)
```

```python
import functools
import math

import jax
import jax.numpy as jnp
from jax import lax
from jax.experimental import pallas as pl
from jax.experimental.pallas import tpu as pltpu

F32 = jnp.float32
BF16 = jnp.bfloat16

HEAD_DIM = 128
ROPE_DIM = HEAD_DIM // 4
ROPE_HALF = ROPE_DIM // 2
ROPE_THETA = 500000.0
POOL_SIZES = (2, 4, 8, 16)
DILATIONS = (1, 4, 16)
ATTN_BLOCK = 128
TOPK = 16
NORM_EPS = 1e-6
NEG_BIG = -1e30

LANES = 128
SUBLANES = 8
VMEM_LIMIT = 56 * 1024 * 1024

ATTN_SPAN = ATTN_BLOCK * max(DILATIONS)
POOL_HALO = 16


def _params(sem, vmem=VMEM_LIMIT):
    return pltpu.CompilerParams(dimension_semantics=sem, vmem_limit_bytes=vmem)


def _mod_kernel(c_ref, w_ref, b_ref, o_ref):
    c = c_ref[...]
    a = c * jax.nn.sigmoid(c)
    o_ref[...] = jnp.dot(a, w_ref[...], preferred_element_type=F32,
                         precision=lax.Precision.HIGHEST) + b_ref[...]


def _modulation(c, w_mod, b_mod):
    B, D = c.shape
    N = w_mod.shape[1]
    tn = 1024
    return pl.pallas_call(
        _mod_kernel,
        grid=(N // tn,),
        in_specs=[pl.BlockSpec((B, D), lambda j: (0, 0)),
                  pl.BlockSpec((D, tn), lambda j: (0, j)),
                  pl.BlockSpec((1, tn), lambda j: (0, j))],
        out_specs=pl.BlockSpec((B, tn), lambda j: (0, j)),
        out_shape=jax.ShapeDtypeStruct((B, N), F32),
        compiler_params=_params(("arbitrary",)),
        name="modulation",
    )(c, w_mod, b_mod.reshape(1, N))


def _rmsnorm_mod(x, g, shift, scale):
    ms = jnp.mean(x * x, axis=-1, keepdims=True)
    y = x * lax.rsqrt(ms + NORM_EPS) * g
    return y * (1.0 + scale) + shift


def _inproj_kernel(x_ref, mod_ref, g_ref, pos_ref, inv_ref, w_ref, wpool_ref, pscale_ref,
                   qkv_ref, pool_ref, h_scr, cos_scr, sina_scr, sinb_scr, carry_scr, ext_scr,
                   *, tm, tiles_per_seq, attn_width):
    i = pl.program_id(0)
    j = pl.program_id(1)
    n_heads = attn_width // HEAD_DIM

    @pl.when(j == 0)
    def _():
        h = _rmsnorm_mod(x_ref[...], g_ref[...], mod_ref[0, 0:1, :], mod_ref[0, 1:2, :])
        h_scr[...] = h.astype(BF16)
        ang = pos_ref[...].astype(F32) * inv_ref[...]
        lane = lax.broadcasted_iota(jnp.int32, ang.shape, 1)
        s = jnp.sin(ang)
        cos_scr[...] = jnp.cos(ang)
        sina_scr[...] = jnp.where(lane < ROPE_HALF, -s, 0.0)
        sinb_scr[...] = jnp.where((lane >= ROPE_HALF) & (lane < ROPE_DIM), s, 0.0)

    z = jnp.dot(h_scr[...], w_ref[...], preferred_element_type=F32)

    def rotary(scale):
        cos = cos_scr[...]
        sina = sina_scr[...]
        sinb = sinb_scr[...]
        for hh in range(n_heads):
            zh = z[:, hh * HEAD_DIM:(hh + 1) * HEAD_DIM]
            up = pltpu.roll(zh, HEAD_DIM - ROPE_HALF, axis=1)
            dn = pltpu.roll(zh, ROPE_HALF, axis=1)
            out = zh * cos + up * sina + dn * sinb
            if scale is not None:
                out = out * scale
            qkv_ref[:, hh * HEAD_DIM:(hh + 1) * HEAD_DIM] = out

    @pl.when(j == 0)
    def _():
        rotary(HEAD_DIM ** -0.5)

    @pl.when(j == 1)
    def _():
        rotary(None)

    @pl.when(j == 2)
    def _():
        qkv_ref[...] = z

    @pl.when(j == 3)
    def _():
        first = (i % tiles_per_seq) == 0

        @pl.when(first)
        def _():
            carry_scr[...] = jnp.zeros_like(carry_scr)

        ext_scr[0:POOL_HALO, :] = carry_scr[...]
        ext_scr[POOL_HALO:POOL_HALO + tm, :] = z
        carry_scr[...] = z[tm - POOL_HALO:tm, :]
        gw = z.shape[1] // len(POOL_SIZES)
        t_in_seq = (i % tiles_per_seq) * tm + lax.broadcasted_iota(jnp.int32, (tm, gw), 0)
        for gi, p in enumerate(POOL_SIZES):
            cols = slice(gi * gw, (gi + 1) * gw)
            u_g = z[:, cols]
            acc = u_g
            for back in range(1, p):
                acc = acc + ext_scr[POOL_HALO - back:POOL_HALO - back + tm, cols]
            cnt = jnp.minimum(t_in_seq + 1, p).astype(F32)
            r = acc / cnt - u_g
            y = jnp.dot(r.astype(BF16), wpool_ref[gi], preferred_element_type=F32)
            pool_ref[:, cols] = (y * pscale_ref[:, cols]).astype(pool_ref.dtype)


def _in_projection(x2d, mod3, norm_g, pos_col, inv_row, w_in, w_pool, pool_scale, *, seq, tm):
    T, D = x2d.shape
    n_in = w_in.shape[1]
    aw = n_in // 4
    tiles_per_seq = seq // tm
    kern = functools.partial(_inproj_kernel, tm=tm, tiles_per_seq=tiles_per_seq, attn_width=aw)
    return pl.pallas_call(
        kern,
        grid=(T // tm, 4),
        in_specs=[
            pl.BlockSpec((tm, D), lambda i, j: (i, 0)),
            pl.BlockSpec((1, 6, D), lambda i, j: (i // tiles_per_seq, 0, 0)),
            pl.BlockSpec((1, D), lambda i, j: (0, 0)),
            pl.BlockSpec((tm, 1), lambda i, j: (i, 0)),
            pl.BlockSpec((1, LANES), lambda i, j: (0, 0)),
            pl.BlockSpec((D, aw), lambda i, j: (0, j)),
            pl.BlockSpec(w_pool.shape, lambda i, j: (0, 0, 0)),
            pl.BlockSpec((1, aw), lambda i, j: (0, 0)),
        ],
        out_specs=[
            pl.BlockSpec((tm, aw), lambda i, j: (i, jnp.minimum(j, 2))),
            pl.BlockSpec((tm, aw), lambda i, j: (i, 0)),
        ],
        out_shape=[jax.ShapeDtypeStruct((T, 3 * aw), F32),
                   jax.ShapeDtypeStruct((T, aw), BF16)],
        scratch_shapes=[
            pltpu.VMEM((tm, D), BF16),
            pltpu.VMEM((tm, LANES), F32),
            pltpu.VMEM((tm, LANES), F32),
            pltpu.VMEM((tm, LANES), F32),
            pltpu.VMEM((POOL_HALO, aw), F32),
            pltpu.VMEM((POOL_HALO + tm, aw), F32),
        ],
        compiler_params=_params(("arbitrary", "arbitrary")),
        name="in_projection",
    )(x2d, mod3, norm_g, pos_col, inv_row, w_in, w_pool, pool_scale)


def _strided(start, size, stride):
    return pl.ds(start, size) if stride == 1 else pl.ds(start, size, stride=stride)


def _attn_kernel(q_ref, kp_ref, kc_ref, vp_ref, vc_ref, o_ref, o_scr, l_scr):
    n = pl.program_id(2)
    blk = ATTN_BLOCK
    qi = lax.broadcasted_iota(jnp.int32, (blk, 2 * blk), 0)
    kj = lax.broadcasted_iota(jnp.int32, (blk, 2 * blk), 1)
    dist = qi + blk - kj
    band = (dist >= 0) & (dist <= blk)
    band_first = band & ((kj >= blk) | (n > 0))

    for pi, d in enumerate(DILATIONS):
        for r in range(d):
            for c in range(ATTN_SPAN // (blk * d)):
                start = r + d * blk * c
                rows = _strided(start, blk, d)
                q = q_ref[rows, :].astype(BF16)
                if c == 0:
                    lo_rows = _strided(ATTN_SPAN + r - d * blk, blk, d)
                    k_lo, v_lo = kp_ref[lo_rows, :], vp_ref[lo_rows, :]
                else:
                    lo_rows = _strided(start - d * blk, blk, d)
                    k_lo, v_lo = kc_ref[lo_rows, :], vc_ref[lo_rows, :]
                k = jnp.concatenate([k_lo, kc_ref[rows, :]], axis=0).astype(BF16)
                v = jnp.concatenate([v_lo, vc_ref[rows, :]], axis=0).astype(BF16)
                s = lax.dot_general(q, k, (((1,), (1,)), ((), ())), preferred_element_type=F32)
                s = jnp.where(band_first if c == 0 else band, s, NEG_BIG)
                m = jnp.max(s, axis=-1, keepdims=True)
                p = jnp.exp(s - m)
                den = jnp.sum(p, axis=-1, keepdims=True)
                o = jnp.dot(p.astype(BF16), v, preferred_element_type=F32) / den
                o_scr[pi, rows, :] = o
                l_scr[pi, rows, :] = jnp.broadcast_to(m + jnp.log(den), (blk, HEAD_DIM))

    l0, l1, l2 = l_scr[0], l_scr[1], l_scr[2]
    lmax = jnp.maximum(jnp.maximum(l0, l1), l2)
    e0, e1, e2 = jnp.exp(l0 - lmax), jnp.exp(l1 - lmax), jnp.exp(l2 - lmax)
    mixed = (e0 * o_scr[0] + e1 * o_scr[1] + e2 * o_scr[2]) / (e0 + e1 + e2)
    o_ref[...] = mixed.astype(o_ref.dtype)


def _attention(qkv, *, batch, seq, attn_width):
    T = qkv.shape[0]
    n_heads = attn_width // HEAD_DIM
    spans = seq // ATTN_SPAN
    blk = (ATTN_SPAN, HEAD_DIM)

    def cur(col0):
        return lambda b, h, n: (b * spans + n, col0 + h)

    def prev(col0):
        return lambda b, h, n: (b * spans + jnp.maximum(n - 1, 0), col0 + h)

    return pl.pallas_call(
        _attn_kernel,
        grid=(batch, n_heads, spans),
        in_specs=[pl.BlockSpec(blk, cur(0)),
                  pl.BlockSpec(blk, prev(n_heads)),
                  pl.BlockSpec(blk, cur(n_heads)),
                  pl.BlockSpec(blk, prev(2 * n_heads)),
                  pl.BlockSpec(blk, cur(2 * n_heads))],
        out_specs=pl.BlockSpec(blk, cur(0)),
        out_shape=jax.ShapeDtypeStruct((T, attn_width), BF16),
        scratch_shapes=[pltpu.VMEM((len(DILATIONS),) + blk, F32),
                        pltpu.VMEM((len(DILATIONS),) + blk, F32)],
        compiler_params=_params(("arbitrary", "arbitrary", "arbitrary")),
        name="dilated_attention",
    )(qkv, qkv, qkv, qkv, qkv)


def _outproj_kernel(attn_ref, pool_ref, x_ref, mod_ref, g_ref, w_ref, x1_ref, h2_ref, *, attn_width):
    mix = jnp.dot(attn_ref[...], w_ref[0:attn_width, :], preferred_element_type=F32)
    mix = mix + jnp.dot(pool_ref[...], w_ref[attn_width:, :], preferred_element_type=F32)
    x1 = x_ref[...] + mod_ref[0, 2:3, :] * mix
    x1_ref[...] = x1
    h2 = _rmsnorm_mod(x1, g_ref[...], mod_ref[0, 3:4, :], mod_ref[0, 4:5, :])
    h2_ref[...] = h2.astype(BF16)


def _out_projection(attn, pool, x2d, mod3, norm_g, w_out, *, seq, tm):
    T, D = x2d.shape
    aw = attn.shape[1]
    pw = pool.shape[1]
    tiles_per_seq = seq // tm
    return pl.pallas_call(
        functools.partial(_outproj_kernel, attn_width=aw),
        grid=(T // tm,),
        in_specs=[
            pl.BlockSpec((tm, aw), lambda i: (i, 0)),
            pl.BlockSpec((tm, pw), lambda i: (i, 0)),
            pl.BlockSpec((tm, D), lambda i: (i, 0)),
            pl.BlockSpec((1, 6, D), lambda i: (i // tiles_per_seq, 0, 0)),
            pl.BlockSpec((1, D), lambda i: (0, 0)),
            pl.BlockSpec((aw + pw, D), lambda i: (0, 0)),
        ],
        out_specs=[pl.BlockSpec((tm, D), lambda i: (i, 0)),
                   pl.BlockSpec((tm, D), lambda i: (i, 0))],
        out_shape=[jax.ShapeDtypeStruct((T, D), F32),
                   jax.ShapeDtypeStruct((T, D), BF16)],
        compiler_params=_params(("arbitrary",)),
        name="out_projection",
    )(attn, pool, x2d, mod3, norm_g, w_out)


def _oddeven_merge_sort_pairs(n):
    pairs = []

    def merge(lo, hi, r):
        step = r * 2
        if step < hi - lo:
            merge(lo, hi, step)
            merge(lo + r, hi, step)
            for k in range(lo + r, hi - r, step):
                pairs.append((k, k + r))
        else:
            pairs.append((lo, lo + r))

    def sort(lo, hi):
        if hi - lo >= 1:
            mid = lo + (hi - lo) // 2
            sort(lo, mid)
            sort(mid + 1, hi)
            merge(lo, hi, 1)

    sort(0, n - 1)
    return pairs


_SORT16 = _oddeven_merge_sort_pairs(TOPK)


def _sort_desc(vals):
    vals = list(vals)
    for a, b in _SORT16:
        hi, lo = jnp.maximum(vals[a], vals[b]), jnp.minimum(vals[a], vals[b])
        vals[a], vals[b] = hi, lo
    return vals


def _merge_top(a_list, b_list):
    n = TOPK
    c = [jnp.maximum(a_list[k], b_list[n - 1 - k]) for k in range(n)]
    stride = n // 2
    while stride >= 1:
        for k in range(n):
            if (k & stride) == 0:
                hi, lo = jnp.maximum(c[k], c[k + stride]), jnp.minimum(c[k], c[k + stride])
                c[k], c[k + stride] = hi, lo
        stride //= 2
    return c


_CAND_ROWS = [[(i, j) for j in range(TOPK) if (i + 1) * (j + 1) <= TOPK] for i in range(TOPK)]


def _route_kernel(h_ref, wq_ref, keys_ref, e1_ref, e2_ref, thr_ref, qp_scr, top_scr,
                  *, tm, n_heads, n_keys):
    qp = jnp.dot(h_ref[...], wq_ref[...], preferred_element_type=F32)
    qp_scr[...] = qp.astype(BF16)
    n_chunks = tm // LANES
    groups = n_keys // SUBLANES
    assert groups == TOPK

    for h in range(n_heads):
        for half, out_ref in ((0, e1_ref), (1, e2_ref)):
            col0 = (2 * h + half) * n_keys
            s_t = lax.dot_general(keys_ref[h, half], qp_scr[:, col0:col0 + n_keys],
                                  (((1,), (1,)), ((), ())), preferred_element_type=F32)
            out_ref[h] = s_t
            for cidx in range(n_chunks):
                lanes = slice(cidx * LANES, (cidx + 1) * LANES)
                blk = s_t[:, lanes]
                vals = _sort_desc([blk[g * SUBLANES:(g + 1) * SUBLANES, :] for g in range(groups)])
                for shift in (4, 2, 1):
                    partner = [pltpu.roll(v, shift, axis=0) for v in vals]
                    vals = _merge_top(vals, partner)
                for k in range(TOPK):
                    top_scr[half, k, h:h + 1, lanes] = vals[k][0:1, :]

    for cidx in range(n_chunks):
        lanes = slice(cidx * LANES, (cidx + 1) * LANES)
        a = [top_scr[0, k, :, lanes] for k in range(TOPK)]
        b = [top_scr[1, k, :, lanes] for k in range(TOPK)]
        ea = [jnp.exp(v - a[0]) for v in a]
        eb = [jnp.exp(v - b[0]) for v in b]
        pad = jnp.full_like(a[0], -1.0)
        best = None
        for row in _CAND_ROWS:
            lst = [ea[i] * eb[j] for (i, j) in row]
            lst = lst + [pad] * (TOPK - len(lst))
            best = lst if best is None else _merge_top(best, lst)
        z = best[0]
        for k in range(1, TOPK):
            z = z + best[k]
        rz = 1.0 / z
        cut = best[TOPK - 1]
        ebn = [v * rz for v in eb]
        thr = None
        for row in _CAND_ROWS:
            for (i, j) in row:
                sel = ea[i] * eb[j] >= cut
                cand = jnp.where(sel, ea[i] * ebn[j], jnp.inf)
                thr = cand if thr is None else jnp.minimum(thr, cand)
        thr_ref[:, lanes] = thr
        top_scr[0, 0, :, lanes] = a[0]
        top_scr[1, 0, :, lanes] = b[0]
        top_scr[1, 1, :, lanes] = rz

    for h in range(n_heads):
        m1 = top_scr[0, 0, h:h + 1, :]
        m2 = top_scr[1, 0, h:h + 1, :]
        rz = top_scr[1, 1, h:h + 1, :]
        e1_ref[h] = jnp.exp(e1_ref[h] - m1)
        e2_ref[h] = jnp.exp(e2_ref[h] - m2) * rz


def _routing(h2, w_query, sub_keys, *, tm):
    T, D = h2.shape
    n_heads, _, n_keys, kd = sub_keys.shape
    qw = w_query.shape[1]
    kern = functools.partial(_route_kernel, tm=tm, n_heads=n_heads, n_keys=n_keys)
    return pl.pallas_call(
        kern,
        grid=(T // tm,),
        in_specs=[pl.BlockSpec((tm, D), lambda i: (i, 0)),
                  pl.BlockSpec((D, qw), lambda i: (0, 0)),
                  pl.BlockSpec(sub_keys.shape, lambda i: (0, 0, 0, 0))],
        out_specs=[pl.BlockSpec((n_heads, n_keys, tm), lambda i: (0, 0, i)),
                   pl.BlockSpec((n_heads, n_keys, tm), lambda i: (0, 0, i)),
                   pl.BlockSpec((n_heads, tm), lambda i: (0, i))],
        out_shape=[jax.ShapeDtypeStruct((n_heads, n_keys, T), F32),
                   jax.ShapeDtypeStruct((n_heads, n_keys, T), F32),
                   jax.ShapeDtypeStruct((n_heads, T), F32)],
        scratch_shapes=[pltpu.VMEM((tm, qw), BF16),
                        pltpu.VMEM((2, TOPK, n_heads, tm), F32)],
        compiler_params=_params(("arbitrary",)),
        name="peer_routing",
    )(h2, w_query, sub_keys)


def _gelu(a):
    return a * (lax.erf(a * (1.0 / math.sqrt(2.0))) + 1.0) * 0.5


def _expert_kernel(h_ref, e1_ref, e2_ref, thr_ref, u_ref, vt_ref, x1_ref, mod_ref, g_ref,
                   o_ref, acc_scr, act_scr, *, tm, te, n_heads, n_keys):
    j = pl.program_id(1)
    n_i1 = te // n_keys
    n_chunks = tm // LANES

    @pl.when(j == 0)
    def _():
        acc_scr[...] = jnp.zeros_like(acc_scr)

    a_t = lax.dot_general(u_ref[...], h_ref[...], (((1,), (1,)), ((), ())),
                          preferred_element_type=F32)
    for k in range(n_i1):
        for cidx in range(n_chunks):
            lanes = slice(cidx * LANES, (cidx + 1) * LANES)
            gate = jnp.zeros((n_keys, LANES), F32)
            for h in range(n_heads):
                e1_row = e1_ref[h, k:k + 1, lanes]
                p = e1_row * e2_ref[h, :, lanes]
                gate = gate + jnp.where(p >= thr_ref[h:h + 1, lanes], p, 0.0)
            a_blk = a_t[k * n_keys:(k + 1) * n_keys, lanes]
            act_scr[k * n_keys:(k + 1) * n_keys, lanes] = (_gelu(a_blk) * gate).astype(BF16)

    acc_scr[...] += jnp.dot(vt_ref[...], act_scr[...], preferred_element_type=F32)

    @pl.when(j == pl.num_programs(1) - 1)
    def _():
        y = acc_scr[...].T
        x2 = x1_ref[...] + mod_ref[0, 5:6, :] * y
        ms = jnp.mean(x2 * x2, axis=-1, keepdims=True)
        o_ref[...] = x2 * lax.rsqrt(ms + NORM_EPS) * g_ref[...]


def _experts(h2, e1, e2, thr, peer_u, peer_vt, x1, mod3, final_g, *, seq, tm, te):
    T, D = h2.shape
    n_heads, n_keys, _ = e1.shape
    E = peer_u.shape[0]
    tiles_per_seq = seq // tm
    kern = functools.partial(_expert_kernel, tm=tm, te=te, n_heads=n_heads, n_keys=n_keys)
    return pl.pallas_call(
        kern,
        grid=(T // tm, E // te),
        in_specs=[
            pl.BlockSpec((tm, D), lambda i, j: (i, 0)),
            pl.BlockSpec((n_heads, te // n_keys, tm), lambda i, j: (0, j, i)),
            pl.BlockSpec((n_heads, n_keys, tm), lambda i, j: (0, 0, i)),
            pl.BlockSpec((n_heads, tm), lambda i, j: (0, i)),
            pl.BlockSpec((te, D), lambda i, j: (j, 0)),
            pl.BlockSpec((D, te), lambda i, j: (0, j)),
            pl.BlockSpec((tm, D), lambda i, j: (i, 0)),
            pl.BlockSpec((1, 6, D), lambda i, j: (i // tiles_per_seq, 0, 0)),
            pl.BlockSpec((1, D), lambda i, j: (0, 0)),
        ],
        out_specs=pl.BlockSpec((tm, D), lambda i, j: (i, 0)),
        out_shape=jax.ShapeDtypeStruct((T, D), F32),
        scratch_shapes=[pltpu.VMEM((D, tm), F32),
                        pltpu.VMEM((te, tm), BF16)],
        compiler_params=_params(("arbitrary", "arbitrary")),
        name="peer_experts",
    )(h2, e1, e2, thr, peer_u, peer_vt, x1, mod3, final_g)


def _layer(x2d, c, pos_col, w_mod, b_mod, norm1_g, w_in, w_pool, pool_scale, w_out, norm2_g,
           w_query, sub_keys, peer_u, peer_v, *, batch, seq, final_g):
    T, D = x2d.shape
    aw = w_in.shape[1] // 4
    mod3 = _modulation(c, w_mod, b_mod).reshape(batch, 6, D)
    inv = ROPE_THETA ** (-jnp.arange(ROPE_HALF, dtype=F32) * 2.0 / ROPE_DIM)
    inv_row = jnp.zeros((1, LANES), F32).at[0, :ROPE_DIM].set(jnp.concatenate([inv, inv]))

    qkv, pool = _in_projection(x2d, mod3, norm1_g.reshape(1, D), pos_col, inv_row,
                               w_in.astype(BF16), w_pool.astype(BF16),
                               pool_scale.reshape(1, -1), seq=seq, tm=min(512, seq))
    attn = _attention(qkv, batch=batch, seq=seq, attn_width=aw)
    x1, h2 = _out_projection(attn, pool, x2d, mod3, norm2_g.reshape(1, D), w_out.astype(BF16),
                             seq=seq, tm=min(512, seq))
    e1, e2, thr = _routing(h2, w_query.astype(BF16), sub_keys.astype(BF16), tm=min(512, seq))
    return x1, h2, e1, e2, thr, mod3


def kernel(x, c, positions, w_mod, b_mod, norm1_g, w_in, w_pool, pool_scale, w_out, norm2_g,
           w_query, sub_keys, peer_u, peer_v, final_g):
    B, S, D = x.shape
    depth = w_mod.shape[0]
    assert S % ATTN_SPAN == 0
    x2d = x.reshape(B * S, D)
    pos_col = positions.reshape(B * S, 1)
    for l in range(depth):
        last = l == depth - 1
        x1, h2, e1, e2, thr, mod3 = _layer(
            x2d, c, pos_col, w_mod[l], b_mod[l], norm1_g[l], w_in[l], w_pool[l], pool_scale[l],
            w_out[l], norm2_g[l], w_query[l], sub_keys[l], peer_u[l], peer_v[l],
            batch=B, seq=S, final_g=final_g)
        gain = final_g if last else None
        assert last, "single-layer stack"
        x2d = _experts(h2, e1, e2, thr, peer_u[l].astype(BF16), peer_v[l].T.astype(BF16), x1, mod3,
                       gain.reshape(1, D), seq=S, tm=min(512, S), te=1024)
    return x2d.reshape(B, S, D)
```

```python
import functools
import math

import jax
import jax.numpy as jnp
from jax import lax
from jax.experimental import pallas as pl
from jax.experimental.pallas import tpu as pltpu

F32 = jnp.float32
BF16 = jnp.bfloat16

HEAD_DIM = 128
ROPE_DIM = HEAD_DIM // 4
ROPE_HALF = ROPE_DIM // 2
ROPE_THETA = 500000.0
POOL_SIZES = (2, 4, 8, 16)
DILATIONS = (1, 4, 16)
ATTN_BLOCK = 128
TOPK = 16
NORM_EPS = 1e-6
NEG_BIG = -1e30

LANES = 128
SUBLANES = 8
VMEM_LIMIT = 56 * 1024 * 1024

ATTN_SPAN = ATTN_BLOCK * max(DILATIONS)
POOL_HALO = 16
EXPERT_SUB = 256


def _params(sem, vmem=VMEM_LIMIT):
    return pltpu.CompilerParams(dimension_semantics=sem, vmem_limit_bytes=vmem)


def _mod_kernel(c_ref, w_ref, b_ref, o_ref):
    c = c_ref[...]
    a = c * jax.nn.sigmoid(c)
    o_ref[...] = jnp.dot(a, w_ref[...], preferred_element_type=F32,
                         precision=lax.Precision.HIGHEST) + b_ref[...]


def _modulation(c, w_mod, b_mod):
    B, D = c.shape
    N = w_mod.shape[1]
    tn = 1024
    return pl.pallas_call(
        _mod_kernel,
        grid=(N // tn,),
        in_specs=[pl.BlockSpec((B, D), lambda j: (0, 0)),
                  pl.BlockSpec((D, tn), lambda j: (0, j)),
                  pl.BlockSpec((1, tn), lambda j: (0, j))],
        out_specs=pl.BlockSpec((B, tn), lambda j: (0, j)),
        out_shape=jax.ShapeDtypeStruct((B, N), F32),
        compiler_params=_params(("arbitrary",)),
        name="modulation",
    )(c, w_mod, b_mod.reshape(1, N))


def _rmsnorm_mod(x, g, shift, scale):
    ms = jnp.mean(x * x, axis=-1, keepdims=True)
    y = x * lax.rsqrt(ms + NORM_EPS) * g
    return y * (1.0 + scale) + shift


def _inproj_kernel(x_ref, mod_ref, g_ref, pos_ref, inv_ref, w_ref, wpool_ref, pscale_ref,
                   qkv_ref, pool_ref, h_scr, cos_scr, sina_scr, sinb_scr, carry_scr, ext_scr,
                   *, tm, tiles_per_seq, attn_width):
    i = pl.program_id(0)
    j = pl.program_id(1)
    n_heads = attn_width // HEAD_DIM

    @pl.when(j == 0)
    def _():
        h = _rmsnorm_mod(x_ref[...], g_ref[...], mod_ref[0, 0:1, :], mod_ref[0, 1:2, :])
        h_scr[...] = h.astype(BF16)
        ang = pos_ref[...].astype(F32) * inv_ref[...]
        lane = lax.broadcasted_iota(jnp.int32, ang.shape, 1)
        s = jnp.sin(ang)
        cos_scr[...] = jnp.cos(ang)
        sina_scr[...] = jnp.where(lane < ROPE_HALF, -s, 0.0)
        sinb_scr[...] = jnp.where((lane >= ROPE_HALF) & (lane < ROPE_DIM), s, 0.0)

    z = jnp.dot(h_scr[...], w_ref[...], preferred_element_type=F32)

    def rotary(scale):
        cos = cos_scr[...]
        sina = sina_scr[...]
        sinb = sinb_scr[...]
        for hh in range(n_heads):
            zh = z[:, hh * HEAD_DIM:(hh + 1) * HEAD_DIM]
            up = pltpu.roll(zh, HEAD_DIM - ROPE_HALF, axis=1)
            dn = pltpu.roll(zh, ROPE_HALF, axis=1)
            out = zh * cos + up * sina + dn * sinb
            if scale is not None:
                out = out * scale
            qkv_ref[:, hh * HEAD_DIM:(hh + 1) * HEAD_DIM] = out

    @pl.when(j == 0)
    def _():
        rotary(HEAD_DIM ** -0.5)

    @pl.when(j == 1)
    def _():
        rotary(None)

    @pl.when(j == 2)
    def _():
        qkv_ref[...] = z

    @pl.when(j == 3)
    def _():
        first = (i % tiles_per_seq) == 0

        @pl.when(first)
        def _():
            carry_scr[...] = jnp.zeros_like(carry_scr)

        ext_scr[0:POOL_HALO, :] = carry_scr[...]
        ext_scr[POOL_HALO:POOL_HALO + tm, :] = z
        carry_scr[...] = z[tm - POOL_HALO:tm, :]
        gw = z.shape[1] // len(POOL_SIZES)
        t_in_seq = (i % tiles_per_seq) * tm + lax.broadcasted_iota(jnp.int32, (tm, gw), 0)
        for gi, p in enumerate(POOL_SIZES):
            cols = slice(gi * gw, (gi + 1) * gw)
            u_g = z[:, cols]
            acc = u_g
            for back in range(1, p):
                acc = acc + ext_scr[POOL_HALO - back:POOL_HALO - back + tm, cols]
            cnt = jnp.minimum(t_in_seq + 1, p).astype(F32)
            r = acc / cnt - u_g
            y = jnp.dot(r.astype(BF16), wpool_ref[gi], preferred_element_type=F32)
            pool_ref[:, cols] = (y * pscale_ref[:, cols]).astype(pool_ref.dtype)


def _in_projection(x2d, mod3, norm_g, pos_col, inv_row, w_in, w_pool, pool_scale, *, seq, tm):
    T, D = x2d.shape
    n_in = w_in.shape[1]
    aw = n_in // 4
    tiles_per_seq = seq // tm
    kern = functools.partial(_inproj_kernel, tm=tm, tiles_per_seq=tiles_per_seq, attn_width=aw)
    return pl.pallas_call(
        kern,
        grid=(T // tm, 4),
        in_specs=[
            pl.BlockSpec((tm, D), lambda i, j: (i, 0)),
            pl.BlockSpec((1, 6, D), lambda i, j: (i // tiles_per_seq, 0, 0)),
            pl.BlockSpec((1, D), lambda i, j: (0, 0)),
            pl.BlockSpec((tm, 1), lambda i, j: (i, 0)),
            pl.BlockSpec((1, LANES), lambda i, j: (0, 0)),
            pl.BlockSpec((D, aw), lambda i, j: (0, j)),
            pl.BlockSpec(w_pool.shape, lambda i, j: (0, 0, 0)),
            pl.BlockSpec((1, aw), lambda i, j: (0, 0)),
        ],
        out_specs=[
            pl.BlockSpec((tm, aw), lambda i, j: (i, jnp.minimum(j, 2))),
            pl.BlockSpec((tm, aw), lambda i, j: (i, 0)),
        ],
        out_shape=[jax.ShapeDtypeStruct((T, 3 * aw), F32),
                   jax.ShapeDtypeStruct((T, aw), BF16)],
        scratch_shapes=[
            pltpu.VMEM((tm, D), BF16),
            pltpu.VMEM((tm, LANES), F32),
            pltpu.VMEM((tm, LANES), F32),
            pltpu.VMEM((tm, LANES), F32),
            pltpu.VMEM((POOL_HALO, aw), F32),
            pltpu.VMEM((POOL_HALO + tm, aw), F32),
        ],
        compiler_params=_params(("arbitrary", "arbitrary")),
        name="in_projection",
    )(x2d, mod3, norm_g, pos_col, inv_row, w_in, w_pool, pool_scale)


def _strided(start, size, stride):
    return pl.ds(start, size) if stride == 1 else pl.ds(start, size, stride=stride)


def _attn_kernel(q_ref, kp_ref, kc_ref, vp_ref, vc_ref, o_ref, o_scr, l_scr):
    n = pl.program_id(2)
    blk = ATTN_BLOCK
    qi = lax.broadcasted_iota(jnp.int32, (blk, 2 * blk), 0)
    kj = lax.broadcasted_iota(jnp.int32, (blk, 2 * blk), 1)
    dist = qi + blk - kj
    band = (dist >= 0) & (dist <= blk)
    band_first = band & ((kj >= blk) | (n > 0))

    for pi, d in enumerate(DILATIONS):
        for r in range(d):
            for c in range(ATTN_SPAN // (blk * d)):
                start = r + d * blk * c
                rows = _strided(start, blk, d)
                q = q_ref[rows, :].astype(BF16)
                if c == 0:
                    lo_rows = _strided(ATTN_SPAN + r - d * blk, blk, d)
                    k_lo, v_lo = kp_ref[lo_rows, :], vp_ref[lo_rows, :]
                else:
                    lo_rows = _strided(start - d * blk, blk, d)
                    k_lo, v_lo = kc_ref[lo_rows, :], vc_ref[lo_rows, :]
                k = jnp.concatenate([k_lo, kc_ref[rows, :]], axis=0).astype(BF16)
                v = jnp.concatenate([v_lo, vc_ref[rows, :]], axis=0).astype(BF16)
                s = lax.dot_general(q, k, (((1,), (1,)), ((), ())), preferred_element_type=F32)
                s = jnp.where(band_first if c == 0 else band, s, NEG_BIG)
                m = jnp.max(s, axis=-1, keepdims=True)
                p = jnp.exp(s - m)
                den = jnp.sum(p, axis=-1, keepdims=True)
                o = jnp.dot(p.astype(BF16), v, preferred_element_type=F32) / den
                o_scr[pi, rows, :] = o
                l_scr[pi, rows, :] = jnp.broadcast_to(m + jnp.log(den), (blk, HEAD_DIM))

    l0, l1, l2 = l_scr[0], l_scr[1], l_scr[2]
    lmax = jnp.maximum(jnp.maximum(l0, l1), l2)
    e0, e1, e2 = jnp.exp(l0 - lmax), jnp.exp(l1 - lmax), jnp.exp(l2 - lmax)
    mixed = (e0 * o_scr[0] + e1 * o_scr[1] + e2 * o_scr[2]) / (e0 + e1 + e2)
    o_ref[...] = mixed.astype(o_ref.dtype)


def _attention(qkv, *, batch, seq, attn_width):
    T = qkv.shape[0]
    n_heads = attn_width // HEAD_DIM
    spans = seq // ATTN_SPAN
    blk = (ATTN_SPAN, HEAD_DIM)

    def cur(col0):
        return lambda b, h, n: (b * spans + n, col0 + h)

    def prev(col0):
        return lambda b, h, n: (b * spans + jnp.maximum(n - 1, 0), col0 + h)

    return pl.pallas_call(
        _attn_kernel,
        grid=(batch, n_heads, spans),
        in_specs=[pl.BlockSpec(blk, cur(0)),
                  pl.BlockSpec(blk, prev(n_heads)),
                  pl.BlockSpec(blk, cur(n_heads)),
                  pl.BlockSpec(blk, prev(2 * n_heads)),
                  pl.BlockSpec(blk, cur(2 * n_heads))],
        out_specs=pl.BlockSpec(blk, cur(0)),
        out_shape=jax.ShapeDtypeStruct((T, attn_width), BF16),
        scratch_shapes=[pltpu.VMEM((len(DILATIONS),) + blk, F32),
                        pltpu.VMEM((len(DILATIONS),) + blk, F32)],
        compiler_params=_params(("arbitrary", "arbitrary", "arbitrary")),
        name="dilated_attention",
    )(qkv, qkv, qkv, qkv, qkv)


def _outproj_kernel(attn_ref, pool_ref, x_ref, mod_ref, g_ref, w_ref, x1_ref, h2_ref, h2t_ref,
                    *, attn_width):
    mix = jnp.dot(attn_ref[...], w_ref[0:attn_width, :], preferred_element_type=F32)
    mix = mix + jnp.dot(pool_ref[...], w_ref[attn_width:, :], preferred_element_type=F32)
    x1 = x_ref[...] + mod_ref[0, 2:3, :] * mix
    x1_ref[...] = x1
    h2 = _rmsnorm_mod(x1, g_ref[...], mod_ref[0, 3:4, :], mod_ref[0, 4:5, :])
    h2_ref[...] = h2.astype(BF16)
    h2t_ref[...] = h2.T.astype(BF16)


def _out_projection(attn, pool, x2d, mod3, norm_g, w_out, *, seq, tm):
    T, D = x2d.shape
    aw = attn.shape[1]
    pw = pool.shape[1]
    tiles_per_seq = seq // tm
    return pl.pallas_call(
        functools.partial(_outproj_kernel, attn_width=aw),
        grid=(T // tm,),
        in_specs=[
            pl.BlockSpec((tm, aw), lambda i: (i, 0)),
            pl.BlockSpec((tm, pw), lambda i: (i, 0)),
            pl.BlockSpec((tm, D), lambda i: (i, 0)),
            pl.BlockSpec((1, 6, D), lambda i: (i // tiles_per_seq, 0, 0)),
            pl.BlockSpec((1, D), lambda i: (0, 0)),
            pl.BlockSpec((aw + pw, D), lambda i: (0, 0)),
        ],
        out_specs=[pl.BlockSpec((tm, D), lambda i: (i, 0)),
                   pl.BlockSpec((tm, D), lambda i: (i, 0)),
                   pl.BlockSpec((D, tm), lambda i: (0, i))],
        out_shape=[jax.ShapeDtypeStruct((T, D), F32),
                   jax.ShapeDtypeStruct((T, D), BF16),
                   jax.ShapeDtypeStruct((D, T), BF16)],
        compiler_params=_params(("arbitrary",)),
        name="out_projection",
    )(attn, pool, x2d, mod3, norm_g, w_out)


def _oddeven_merge_sort_pairs(n):
    pairs = []

    def merge(lo, hi, r):
        step = r * 2
        if step < hi - lo:
            merge(lo, hi, step)
            merge(lo + r, hi, step)
            for k in range(lo + r, hi - r, step):
                pairs.append((k, k + r))
        else:
            pairs.append((lo, lo + r))

    def sort(lo, hi):
        if hi - lo >= 1:
            mid = lo + (hi - lo) // 2
            sort(lo, mid)
            sort(mid + 1, hi)
            merge(lo, hi, 1)

    sort(0, n - 1)
    return pairs


_SORT16 = _oddeven_merge_sort_pairs(TOPK)


def _sort_desc(vals):
    vals = list(vals)
    for a, b in _SORT16:
        hi, lo = jnp.maximum(vals[a], vals[b]), jnp.minimum(vals[a], vals[b])
        vals[a], vals[b] = hi, lo
    return vals


def _merge_top(a_list, b_list):
    n = TOPK
    c = [jnp.maximum(a_list[k], b_list[n - 1 - k]) for k in range(n)]
    stride = n // 2
    while stride >= 1:
        for k in range(n):
            if (k & stride) == 0:
                hi, lo = jnp.maximum(c[k], c[k + stride]), jnp.minimum(c[k], c[k + stride])
                c[k], c[k + stride] = hi, lo
        stride //= 2
    return c


_CAND_ROWS = [[(i, j) for j in range(TOPK) if (i + 1) * (j + 1) <= TOPK] for i in range(TOPK)]


def _route_kernel(h_ref, wq_ref, keys_ref, e1_ref, e2_ref, thr_ref, qp_scr, top_scr,
                  *, tm, n_heads, n_keys):
    qp = jnp.dot(h_ref[...], wq_ref[...], preferred_element_type=F32)
    qp_scr[...] = qp.astype(BF16)
    n_chunks = tm // LANES
    groups = n_keys // SUBLANES
    assert groups == TOPK

    for h in range(n_heads):
        for half, out_ref in ((0, e1_ref), (1, e2_ref)):
            col0 = (2 * h + half) * n_keys
            s_t = lax.dot_general(keys_ref[h, half], qp_scr[:, col0:col0 + n_keys],
                                  (((1,), (1,)), ((), ())), preferred_element_type=F32)
            out_ref[h] = s_t
            for cidx in range(n_chunks):
                lanes = slice(cidx * LANES, (cidx + 1) * LANES)
                blk = s_t[:, lanes]
                vals = _sort_desc([blk[g * SUBLANES:(g + 1) * SUBLANES, :] for g in range(groups)])
                for shift in (4, 2, 1):
                    partner = [pltpu.roll(v, shift, axis=0) for v in vals]
                    vals = _merge_top(vals, partner)
                for k in range(TOPK):
                    top_scr[half, k, h:h + 1, lanes] = vals[k][0:1, :]

    for cidx in range(n_chunks):
        lanes = slice(cidx * LANES, (cidx + 1) * LANES)
        a = [top_scr[0, k, :, lanes] for k in range(TOPK)]
        b = [top_scr[1, k, :, lanes] for k in range(TOPK)]
        ea = [jnp.exp(v - a[0]) for v in a]
        eb = [jnp.exp(v - b[0]) for v in b]
        pad = jnp.full_like(a[0], -1.0)
        best = None
        for row in _CAND_ROWS:
            lst = [ea[i] * eb[j] for (i, j) in row]
            lst = lst + [pad] * (TOPK - len(lst))
            best = lst if best is None else _merge_top(best, lst)
        z = best[0]
        for k in range(1, TOPK):
            z = z + best[k]
        rz = 1.0 / z
        cut = best[TOPK - 1]
        ebn = [v * rz for v in eb]
        thr = None
        for row in _CAND_ROWS:
            for (i, j) in row:
                sel = ea[i] * eb[j] >= cut
                cand = jnp.where(sel, ea[i] * ebn[j], jnp.inf)
                thr = cand if thr is None else jnp.minimum(thr, cand)
        thr_ref[:, lanes] = thr
        top_scr[0, 0, :, lanes] = a[0]
        top_scr[1, 0, :, lanes] = b[0]
        top_scr[1, 1, :, lanes] = rz

    for h in range(n_heads):
        m1 = top_scr[0, 0, h:h + 1, :]
        m2 = top_scr[1, 0, h:h + 1, :]
        rz = top_scr[1, 1, h:h + 1, :]
        e1_ref[h] = jnp.exp(e1_ref[h] - m1)
        e2_ref[h] = jnp.exp(e2_ref[h] - m2) * rz


def _routing(h2, w_query, sub_keys, *, tm):
    T, D = h2.shape
    n_heads, _, n_keys, kd = sub_keys.shape
    qw = w_query.shape[1]
    kern = functools.partial(_route_kernel, tm=tm, n_heads=n_heads, n_keys=n_keys)
    key_blk = pl.BlockSpec((n_heads, n_keys, tm), lambda i: (0, 0, i))
    return pl.pallas_call(
        kern,
        grid=(T // tm,),
        in_specs=[pl.BlockSpec((tm, D), lambda i: (i, 0)),
                  pl.BlockSpec((D, qw), lambda i: (0, 0)),
                  pl.BlockSpec(sub_keys.shape, lambda i: (0, 0, 0, 0))],
        out_specs=[key_blk, key_blk, pl.BlockSpec((n_heads, tm), lambda i: (0, i))],
        out_shape=[jax.ShapeDtypeStruct((n_heads, n_keys, T), F32),
                   jax.ShapeDtypeStruct((n_heads, n_keys, T), F32),
                   jax.ShapeDtypeStruct((n_heads, T), F32)],
        scratch_shapes=[pltpu.VMEM((tm, qw), BF16),
                        pltpu.VMEM((2, TOPK, n_heads, tm), F32)],
        compiler_params=_params(("arbitrary",)),
        name="peer_routing",
    )(h2, w_query, sub_keys)


def _gelu(a):
    return a * (lax.erf(a * (1.0 / math.sqrt(2.0))) + 1.0) * 0.5


def _expert_kernel(ht_ref, e1_ref, e2_ref, thr_ref, u_ref, vt_ref, x1_ref, mod_ref, g_ref,
                   o_ref, acc_scr, act_scr, pre_scr, *, tm, te, n_heads, n_keys):
    j = pl.program_id(1)
    n_i1 = te // n_keys
    n_chunks = tm // LANES

    @pl.when(j == 0)
    def _():
        acc_scr[...] = jnp.zeros_like(acc_scr)

    n_sub = te // EXPERT_SUB
    i1_per_sub = EXPERT_SUB // n_keys
    n_groups = n_keys // SUBLANES

    def pre_activation(s):
        rows = slice(s * EXPERT_SUB, (s + 1) * EXPERT_SUB)
        pre_scr[rows, :] = jnp.dot(u_ref[rows, :], ht_ref[...], preferred_element_type=F32)

    def activation(s):
        for k in range(s * i1_per_sub, (s + 1) * i1_per_sub):
            for cidx in range(n_chunks):
                lanes = slice(cidx * LANES, (cidx + 1) * LANES)
                g = _gelu(pre_scr[k * n_keys:(k + 1) * n_keys, lanes])
                tie = g[0:SUBLANES, :] * 0.0
                gate = [None] * n_groups
                for h in range(n_heads):
                    e1_rep = e1_ref[h, k:k + 1, lanes] + tie
                    thr_row = thr_ref[h:h + 1, lanes]
                    for r in range(n_groups):
                        p = e1_rep * e2_ref[h, r * SUBLANES:(r + 1) * SUBLANES, lanes]
                        sel = jnp.where(p >= thr_row, p, 0.0)
                        gate[r] = sel if gate[r] is None else gate[r] + sel
                gate = jnp.concatenate(gate, axis=0)
                act_scr[k * n_keys:(k + 1) * n_keys, lanes] = (g * gate).astype(BF16)

    def contract(s):
        rows = slice(s * EXPERT_SUB, (s + 1) * EXPERT_SUB)
        acc_scr[...] += jnp.dot(vt_ref[:, rows], act_scr[rows, :], preferred_element_type=F32)

    for s in range(n_sub):
        pre_activation(s)
    for s in range(n_sub):
        activation(s)
        contract(s)

    @pl.when(j == pl.num_programs(1) - 1)
    def _():
        y = acc_scr[...].T
        x2 = x1_ref[...] + mod_ref[0, 5:6, :] * y
        ms = jnp.mean(x2 * x2, axis=-1, keepdims=True)
        o_ref[...] = x2 * lax.rsqrt(ms + NORM_EPS) * g_ref[...]


def _experts(h2t, e1, e2, thr, peer_u, peer_vt, x1, mod3, final_g, *, seq, tm, te):
    D, T = h2t.shape
    n_heads, n_keys, _ = e1.shape
    E = peer_u.shape[0]
    tiles_per_seq = seq // tm
    n_i1 = te // n_keys
    assert n_i1 == SUBLANES, "expert rows of one block fill the sublanes of an f32 tile"
    kern = functools.partial(_expert_kernel, tm=tm, te=te, n_heads=n_heads, n_keys=n_keys)
    row_blk = pl.BlockSpec((n_heads, n_i1, tm), lambda i, j: (0, j, i))
    key_blk = pl.BlockSpec((n_heads, n_keys, tm), lambda i, j: (0, 0, i))
    return pl.pallas_call(
        kern,
        grid=(T // tm, E // te),
        in_specs=[
            pl.BlockSpec((D, tm), lambda i, j: (0, i)),
            row_blk, key_blk,
            pl.BlockSpec((n_heads, tm), lambda i, j: (0, i)),
            pl.BlockSpec((te, D), lambda i, j: (j, 0)),
            pl.BlockSpec((D, te), lambda i, j: (0, j)),
            pl.BlockSpec((tm, D), lambda i, j: (i, 0)),
            pl.BlockSpec((1, 6, D), lambda i, j: (i // tiles_per_seq, 0, 0)),
            pl.BlockSpec((1, D), lambda i, j: (0, 0)),
        ],
        out_specs=pl.BlockSpec((tm, D), lambda i, j: (i, 0)),
        out_shape=jax.ShapeDtypeStruct((T, D), F32),
        scratch_shapes=[pltpu.VMEM((D, tm), F32),
                        pltpu.VMEM((te, tm), BF16),
                        pltpu.VMEM((te, tm), F32)],
        compiler_params=_params(("arbitrary", "arbitrary")),
        name="peer_experts",
    )(h2t, e1, e2, thr, peer_u, peer_vt, x1, mod3, final_g)


def kernel(x, c, positions, w_mod, b_mod, norm1_g, w_in, w_pool, pool_scale, w_out, norm2_g,
           w_query, sub_keys, peer_u, peer_v, final_g):
    B, S, D = x.shape
    assert w_mod.shape[0] == 1, "single-layer stack"
    assert S % ATTN_SPAN == 0
    tm = min(512, S)
    aw = w_in.shape[2] // 4
    x2d = x.reshape(B * S, D)
    pos_col = positions.reshape(B * S, 1)
    inv = ROPE_THETA ** (-jnp.arange(ROPE_HALF, dtype=F32) * 2.0 / ROPE_DIM)
    inv_row = jnp.zeros((1, LANES), F32).at[0, :ROPE_DIM].set(jnp.concatenate([inv, inv]))

    mod3 = _modulation(c, w_mod[0], b_mod[0]).reshape(B, 6, D)
    qkv, pool = _in_projection(x2d, mod3, norm1_g[0].reshape(1, D), pos_col, inv_row,
                               w_in[0].astype(BF16), w_pool[0].astype(BF16),
                               pool_scale[0].reshape(1, -1), seq=S, tm=tm)
    attn = _attention(qkv, batch=B, seq=S, attn_width=aw)
    x1, h2, h2t = _out_projection(attn, pool, x2d, mod3, norm2_g[0].reshape(1, D),
                                  w_out[0].astype(BF16), seq=S, tm=tm)
    e1, e2, thr = _routing(h2, w_query[0].astype(BF16), sub_keys[0].astype(BF16), tm=tm)
    out = _experts(h2t, e1, e2, thr, peer_u[0].astype(BF16), peer_v[0].T.astype(BF16),
                   x1, mod3, final_g.reshape(1, D), seq=S, tm=tm, te=1024)
    return out.reshape(B, S, D)
```

```python
import functools
import math

import jax
import jax.numpy as jnp
from jax import lax
from jax.experimental import pallas as pl
from jax.experimental.pallas import tpu as pltpu

F32 = jnp.float32
BF16 = jnp.bfloat16

HEAD_DIM = 128
ROPE_DIM = HEAD_DIM // 4
ROPE_HALF = ROPE_DIM // 2
ROPE_THETA = 500000.0
POOL_SIZES = (2, 4, 8, 16)
DILATIONS = (1, 4, 16)
ATTN_BLOCK = 128
TOPK = 16
NORM_EPS = 1e-6
NEG_BIG = -1e30

LANES = 128
SUBLANES = 8
VMEM_LIMIT = 56 * 1024 * 1024

ATTN_SPAN = ATTN_BLOCK * max(DILATIONS)
POOL_HALO = 16
EXPERT_SUB = 256


def _params(sem, vmem=VMEM_LIMIT):
    return pltpu.CompilerParams(dimension_semantics=sem, vmem_limit_bytes=vmem)


def _mod_kernel(c_ref, w_ref, b_ref, o_ref):
    c = c_ref[...]
    a = c * jax.nn.sigmoid(c)
    o_ref[...] = jnp.dot(a, w_ref[...], preferred_element_type=F32,
                         precision=lax.Precision.HIGHEST) + b_ref[...]


def _modulation(c, w_mod, b_mod):
    B, D = c.shape
    N = w_mod.shape[1]
    tn = 1024
    return pl.pallas_call(
        _mod_kernel,
        grid=(N // tn,),
        in_specs=[pl.BlockSpec((B, D), lambda j: (0, 0)),
                  pl.BlockSpec((D, tn), lambda j: (0, j)),
                  pl.BlockSpec((1, tn), lambda j: (0, j))],
        out_specs=pl.BlockSpec((B, tn), lambda j: (0, j)),
        out_shape=jax.ShapeDtypeStruct((B, N), F32),
        compiler_params=_params(("arbitrary",)),
        name="modulation",
    )(c, w_mod, b_mod.reshape(1, N))


def _rmsnorm_mod(x, g, shift, scale):
    ms = jnp.mean(x * x, axis=-1, keepdims=True)
    y = x * lax.rsqrt(ms + NORM_EPS) * g
    return y * (1.0 + scale) + shift


def _inproj_kernel(x_ref, mod_ref, g_ref, pos_ref, inv_ref, w_ref, wpool_ref, pscale_ref,
                   qkv_ref, pool_ref, h_scr, cos_scr, sina_scr, sinb_scr, carry_scr, ext_scr,
                   *, tm, tiles_per_seq, attn_width):
    i = pl.program_id(0)
    j = pl.program_id(1)
    n_heads = attn_width // HEAD_DIM

    @pl.when(j == 0)
    def _():
        h = _rmsnorm_mod(x_ref[...], g_ref[...], mod_ref[0, 0:1, :], mod_ref[0, 1:2, :])
        h_scr[...] = h.astype(BF16)
        ang = pos_ref[...].astype(F32) * inv_ref[...]
        lane = lax.broadcasted_iota(jnp.int32, ang.shape, 1)
        s = jnp.sin(ang)
        cos_scr[...] = jnp.cos(ang)
        sina_scr[...] = jnp.where(lane < ROPE_HALF, -s, 0.0)
        sinb_scr[...] = jnp.where((lane >= ROPE_HALF) & (lane < ROPE_DIM), s, 0.0)

    z = jnp.dot(h_scr[...], w_ref[...], preferred_element_type=F32)

    def rotary(scale):
        cos = cos_scr[...]
        sina = sina_scr[...]
        sinb = sinb_scr[...]
        for hh in range(n_heads):
            zh = z[:, hh * HEAD_DIM:(hh + 1) * HEAD_DIM]
            up = pltpu.roll(zh, HEAD_DIM - ROPE_HALF, axis=1)
            dn = pltpu.roll(zh, ROPE_HALF, axis=1)
            out = zh * cos + up * sina + dn * sinb
            if scale is not None:
                out = out * scale
            qkv_ref[:, hh * HEAD_DIM:(hh + 1) * HEAD_DIM] = out

    @pl.when(j == 0)
    def _():
        rotary(HEAD_DIM ** -0.5)

    @pl.when(j == 1)
    def _():
        rotary(None)

    @pl.when(j == 2)
    def _():
        qkv_ref[...] = z

    @pl.when(j == 3)
    def _():
        first = (i % tiles_per_seq) == 0

        @pl.when(first)
        def _():
            carry_scr[...] = jnp.zeros_like(carry_scr)

        ext_scr[0:POOL_HALO, :] = carry_scr[...]
        ext_scr[POOL_HALO:POOL_HALO + tm, :] = z
        carry_scr[...] = z[tm - POOL_HALO:tm, :]
        gw = z.shape[1] // len(POOL_SIZES)
        t_in_seq = (i % tiles_per_seq) * tm + lax.broadcasted_iota(jnp.int32, (tm, gw), 0)
        for gi, p in enumerate(POOL_SIZES):
            cols = slice(gi * gw, (gi + 1) * gw)
            u_g = z[:, cols]
            acc = u_g
            for back in range(1, p):
                acc = acc + ext_scr[POOL_HALO - back:POOL_HALO - back + tm, cols]
            cnt = jnp.minimum(t_in_seq + 1, p).astype(F32)
            r = acc / cnt - u_g
            y = jnp.dot(r.astype(BF16), wpool_ref[gi], preferred_element_type=F32)
            pool_ref[:, cols] = (y * pscale_ref[:, cols]).astype(pool_ref.dtype)


def _in_projection(x2d, mod3, norm_g, pos_col, inv_row, w_in, w_pool, pool_scale, *, seq, tm):
    T, D = x2d.shape
    n_in = w_in.shape[1]
    aw = n_in // 4
    tiles_per_seq = seq // tm
    kern = functools.partial(_inproj_kernel, tm=tm, tiles_per_seq=tiles_per_seq, attn_width=aw)
    return pl.pallas_call(
        kern,
        grid=(T // tm, 4),
        in_specs=[
            pl.BlockSpec((tm, D), lambda i, j: (i, 0)),
            pl.BlockSpec((1, 6, D), lambda i, j: (i // tiles_per_seq, 0, 0)),
            pl.BlockSpec((1, D), lambda i, j: (0, 0)),
            pl.BlockSpec((tm, 1), lambda i, j: (i, 0)),
            pl.BlockSpec((1, LANES), lambda i, j: (0, 0)),
            pl.BlockSpec((D, aw), lambda i, j: (0, j)),
            pl.BlockSpec(w_pool.shape, lambda i, j: (0, 0, 0)),
            pl.BlockSpec((1, aw), lambda i, j: (0, 0)),
        ],
        out_specs=[
            pl.BlockSpec((tm, aw), lambda i, j: (i, jnp.minimum(j, 2))),
            pl.BlockSpec((tm, aw), lambda i, j: (i, 0)),
        ],
        out_shape=[jax.ShapeDtypeStruct((T, 3 * aw), F32),
                   jax.ShapeDtypeStruct((T, aw), BF16)],
        scratch_shapes=[
            pltpu.VMEM((tm, D), BF16),
            pltpu.VMEM((tm, LANES), F32),
            pltpu.VMEM((tm, LANES), F32),
            pltpu.VMEM((tm, LANES), F32),
            pltpu.VMEM((POOL_HALO, aw), F32),
            pltpu.VMEM((POOL_HALO + tm, aw), F32),
        ],
        compiler_params=_params(("arbitrary", "arbitrary")),
        name="in_projection",
    )(x2d, mod3, norm_g, pos_col, inv_row, w_in, w_pool, pool_scale)


def _strided(start, size, stride):
    return pl.ds(start, size) if stride == 1 else pl.ds(start, size, stride=stride)


def _attn_kernel(q_ref, kp_ref, kc_ref, vp_ref, vc_ref, o_ref, o_scr, l_scr):
    n = pl.program_id(2)
    blk = ATTN_BLOCK
    qi = lax.broadcasted_iota(jnp.int32, (blk, 2 * blk), 0)
    kj = lax.broadcasted_iota(jnp.int32, (blk, 2 * blk), 1)
    dist = qi + blk - kj
    band = (dist >= 0) & (dist <= blk)
    band_first = band & ((kj >= blk) | (n > 0))

    for pi, d in enumerate(DILATIONS):
        for r in range(d):
            for c in range(ATTN_SPAN // (blk * d)):
                start = r + d * blk * c
                rows = _strided(start, blk, d)
                q = q_ref[rows, :].astype(BF16)
                if c == 0:
                    lo_rows = _strided(ATTN_SPAN + r - d * blk, blk, d)
                    k_lo, v_lo = kp_ref[lo_rows, :], vp_ref[lo_rows, :]
                else:
                    lo_rows = _strided(start - d * blk, blk, d)
                    k_lo, v_lo = kc_ref[lo_rows, :], vc_ref[lo_rows, :]
                k = jnp.concatenate([k_lo, kc_ref[rows, :]], axis=0).astype(BF16)
                v = jnp.concatenate([v_lo, vc_ref[rows, :]], axis=0).astype(BF16)
                s = lax.dot_general(q, k, (((1,), (1,)), ((), ())), preferred_element_type=F32)
                s = jnp.where(band_first if c == 0 else band, s, NEG_BIG)
                m = jnp.max(s, axis=-1, keepdims=True)
                p = jnp.exp(s - m)
                den = jnp.sum(p, axis=-1, keepdims=True)
                o = jnp.dot(p.astype(BF16), v, preferred_element_type=F32) / den
                o_scr[pi, rows, :] = o
                l_scr[pi, rows, :] = jnp.broadcast_to(m + jnp.log(den), (blk, HEAD_DIM))

    l0, l1, l2 = l_scr[0], l_scr[1], l_scr[2]
    lmax = jnp.maximum(jnp.maximum(l0, l1), l2)
    e0, e1, e2 = jnp.exp(l0 - lmax), jnp.exp(l1 - lmax), jnp.exp(l2 - lmax)
    mixed = (e0 * o_scr[0] + e1 * o_scr[1] + e2 * o_scr[2]) / (e0 + e1 + e2)
    o_ref[...] = mixed.astype(o_ref.dtype)


def _attention(qkv, *, batch, seq, attn_width):
    T = qkv.shape[0]
    n_heads = attn_width // HEAD_DIM
    spans = seq // ATTN_SPAN
    blk = (ATTN_SPAN, HEAD_DIM)

    def cur(col0):
        return lambda b, h, n: (b * spans + n, col0 + h)

    def prev(col0):
        return lambda b, h, n: (b * spans + jnp.maximum(n - 1, 0), col0 + h)

    return pl.pallas_call(
        _attn_kernel,
        grid=(batch, n_heads, spans),
        in_specs=[pl.BlockSpec(blk, cur(0)),
                  pl.BlockSpec(blk, prev(n_heads)),
                  pl.BlockSpec(blk, cur(n_heads)),
                  pl.BlockSpec(blk, prev(2 * n_heads)),
                  pl.BlockSpec(blk, cur(2 * n_heads))],
        out_specs=pl.BlockSpec(blk, cur(0)),
        out_shape=jax.ShapeDtypeStruct((T, attn_width), BF16),
        scratch_shapes=[pltpu.VMEM((len(DILATIONS),) + blk, F32),
                        pltpu.VMEM((len(DILATIONS),) + blk, F32)],
        compiler_params=_params(("arbitrary", "arbitrary", "arbitrary")),
        name="dilated_attention",
    )(qkv, qkv, qkv, qkv, qkv)


def _outproj_kernel(attn_ref, pool_ref, x_ref, mod_ref, g_ref, w_ref, x1_ref, h2_ref, h2t_ref,
                    *, attn_width):
    mix = jnp.dot(attn_ref[...], w_ref[0:attn_width, :], preferred_element_type=F32)
    mix = mix + jnp.dot(pool_ref[...], w_ref[attn_width:, :], preferred_element_type=F32)
    x1 = x_ref[...] + mod_ref[0, 2:3, :] * mix
    x1_ref[...] = x1
    h2 = _rmsnorm_mod(x1, g_ref[...], mod_ref[0, 3:4, :], mod_ref[0, 4:5, :])
    h2_ref[...] = h2.astype(BF16)
    h2t_ref[...] = h2.T.astype(BF16)


def _out_projection(attn, pool, x2d, mod3, norm_g, w_out, *, seq, tm):
    T, D = x2d.shape
    aw = attn.shape[1]
    pw = pool.shape[1]
    tiles_per_seq = seq // tm
    return pl.pallas_call(
        functools.partial(_outproj_kernel, attn_width=aw),
        grid=(T // tm,),
        in_specs=[
            pl.BlockSpec((tm, aw), lambda i: (i, 0)),
            pl.BlockSpec((tm, pw), lambda i: (i, 0)),
            pl.BlockSpec((tm, D), lambda i: (i, 0)),
            pl.BlockSpec((1, 6, D), lambda i: (i // tiles_per_seq, 0, 0)),
            pl.BlockSpec((1, D), lambda i: (0, 0)),
            pl.BlockSpec((aw + pw, D), lambda i: (0, 0)),
        ],
        out_specs=[pl.BlockSpec((tm, D), lambda i: (i, 0)),
                   pl.BlockSpec((tm, D), lambda i: (i, 0)),
                   pl.BlockSpec((D, tm), lambda i: (0, i))],
        out_shape=[jax.ShapeDtypeStruct((T, D), F32),
                   jax.ShapeDtypeStruct((T, D), BF16),
                   jax.ShapeDtypeStruct((D, T), BF16)],
        compiler_params=_params(("arbitrary",)),
        name="out_projection",
    )(attn, pool, x2d, mod3, norm_g, w_out)


def _oddeven_merge_sort_pairs(n):
    pairs = []

    def merge(lo, hi, r):
        step = r * 2
        if step < hi - lo:
            merge(lo, hi, step)
            merge(lo + r, hi, step)
            for k in range(lo + r, hi - r, step):
                pairs.append((k, k + r))
        else:
            pairs.append((lo, lo + r))

    def sort(lo, hi):
        if hi - lo >= 1:
            mid = lo + (hi - lo) // 2
            sort(lo, mid)
            sort(mid + 1, hi)
            merge(lo, hi, 1)

    sort(0, n - 1)
    return pairs


_SORT16 = _oddeven_merge_sort_pairs(TOPK)


def _sort_desc(vals):
    vals = list(vals)
    for a, b in _SORT16:
        hi, lo = jnp.maximum(vals[a], vals[b]), jnp.minimum(vals[a], vals[b])
        vals[a], vals[b] = hi, lo
    return vals


def _merge_top(a_list, b_list):
    n = TOPK
    c = [jnp.maximum(a_list[k], b_list[n - 1 - k]) for k in range(n)]
    stride = n // 2
    while stride >= 1:
        for k in range(n):
            if (k & stride) == 0:
                hi, lo = jnp.maximum(c[k], c[k + stride]), jnp.minimum(c[k], c[k + stride])
                c[k], c[k + stride] = hi, lo
        stride //= 2
    return c


_CAND_ROWS = [[(i, j) for j in range(TOPK) if (i + 1) * (j + 1) <= TOPK] for i in range(TOPK)]


def _route_kernel(h_ref, wq_ref, keys_ref, e1_ref, e2_ref, thr_ref, qp_scr, top_scr,
                  *, tm, n_heads, n_keys):
    qp = jnp.dot(h_ref[...], wq_ref[...], preferred_element_type=F32)
    qp_scr[...] = qp.astype(BF16)
    n_chunks = tm // LANES
    groups = n_keys // SUBLANES
    assert groups == TOPK

    for h in range(n_heads):
        for half, out_ref in ((0, e1_ref), (1, e2_ref)):
            col0 = (2 * h + half) * n_keys
            s_t = lax.dot_general(keys_ref[h, half], qp_scr[:, col0:col0 + n_keys],
                                  (((1,), (1,)), ((), ())), preferred_element_type=F32)
            out_ref[h] = s_t
            for cidx in range(n_chunks):
                lanes = slice(cidx * LANES, (cidx + 1) * LANES)
                blk = s_t[:, lanes]
                vals = _sort_desc([blk[g * SUBLANES:(g + 1) * SUBLANES, :] for g in range(groups)])
                for shift in (4, 2, 1):
                    partner = [pltpu.roll(v, shift, axis=0) for v in vals]
                    vals = _merge_top(vals, partner)
                for k in range(TOPK):
                    top_scr[half, k, h:h + 1, lanes] = vals[k][0:1, :]

    for cidx in range(n_chunks):
        lanes = slice(cidx * LANES, (cidx + 1) * LANES)
        a = [top_scr[0, k, :, lanes] for k in range(TOPK)]
        b = [top_scr[1, k, :, lanes] for k in range(TOPK)]
        ea = [jnp.exp(v - a[0]) for v in a]
        eb = [jnp.exp(v - b[0]) for v in b]
        pad = jnp.full_like(a[0], -1.0)
        best = None
        for row in _CAND_ROWS:
            lst = [ea[i] * eb[j] for (i, j) in row]
            lst = lst + [pad] * (TOPK - len(lst))
            best = lst if best is None else _merge_top(best, lst)
        z = best[0]
        for k in range(1, TOPK):
            z = z + best[k]
        rz = 1.0 / z
        cut = best[TOPK - 1]
        ebn = [v * rz for v in eb]
        thr = None
        for row in _CAND_ROWS:
            for (i, j) in row:
                sel = ea[i] * eb[j] >= cut
                cand = jnp.where(sel, ea[i] * ebn[j], jnp.inf)
                thr = cand if thr is None else jnp.minimum(thr, cand)
        thr_ref[:, lanes] = thr
        top_scr[0, 0, :, lanes] = a[0]
        top_scr[1, 0, :, lanes] = b[0]
        top_scr[1, 1, :, lanes] = rz

    for h in range(n_heads):
        m1 = top_scr[0, 0, h:h + 1, :]
        m2 = top_scr[1, 0, h:h + 1, :]
        rz = top_scr[1, 1, h:h + 1, :]
        e1_ref[h] = jnp.exp(e1_ref[h] - m1)
        e2_ref[h] = jnp.exp(e2_ref[h] - m2) * rz


def _routing(h2, w_query, sub_keys, *, tm):
    T, D = h2.shape
    n_heads, _, n_keys, kd = sub_keys.shape
    qw = w_query.shape[1]
    kern = functools.partial(_route_kernel, tm=tm, n_heads=n_heads, n_keys=n_keys)
    key_blk = pl.BlockSpec((n_heads, n_keys, tm), lambda i: (0, 0, i))
    return pl.pallas_call(
        kern,
        grid=(T // tm,),
        in_specs=[pl.BlockSpec((tm, D), lambda i: (i, 0)),
                  pl.BlockSpec((D, qw), lambda i: (0, 0)),
                  pl.BlockSpec(sub_keys.shape, lambda i: (0, 0, 0, 0))],
        out_specs=[key_blk, key_blk, pl.BlockSpec((n_heads, tm), lambda i: (0, i))],
        out_shape=[jax.ShapeDtypeStruct((n_heads, n_keys, T), F32),
                   jax.ShapeDtypeStruct((n_heads, n_keys, T), F32),
                   jax.ShapeDtypeStruct((n_heads, T), F32)],
        scratch_shapes=[pltpu.VMEM((tm, qw), BF16),
                        pltpu.VMEM((2, TOPK, n_heads, tm), F32)],
        compiler_params=_params(("arbitrary",)),
        name="peer_routing",
    )(h2, w_query, sub_keys)


def _gelu(a):
    return a * (lax.erf(a * (1.0 / math.sqrt(2.0))) + 1.0) * 0.5


def _expert_kernel(ht_ref, e1_ref, e2_ref, thr_ref, u_ref, vt_ref, x1_ref, mod_ref, g_ref,
                   o_ref, acc_scr, act_scr, pre_scr, *, tm, te, n_heads, n_keys):
    j = pl.program_id(1)
    n_i1 = te // n_keys
    n_chunks = tm // LANES

    @pl.when(j == 0)
    def _():
        acc_scr[...] = jnp.zeros_like(acc_scr)

    n_sub = te // EXPERT_SUB
    i1_per_sub = EXPERT_SUB // n_keys
    n_groups = n_keys // SUBLANES

    def pre_activation(s):
        rows = slice(s * EXPERT_SUB, (s + 1) * EXPERT_SUB)
        pre_scr[rows, :] = jnp.dot(u_ref[rows, :], ht_ref[...], preferred_element_type=F32)

    def activation(s):
        for k in range(s * i1_per_sub, (s + 1) * i1_per_sub):
            for cidx in range(n_chunks):
                lanes = slice(cidx * LANES, (cidx + 1) * LANES)
                g = _gelu(pre_scr[k * n_keys:(k + 1) * n_keys, lanes])
                tie = g[0:SUBLANES, :] * 0.0
                gate = [None] * n_groups
                for h in range(n_heads):
                    e1_rep = e1_ref[h, k:k + 1, lanes] + tie
                    thr_row = thr_ref[h:h + 1, lanes]
                    for r in range(n_groups):
                        p = e1_rep * e2_ref[h, r * SUBLANES:(r + 1) * SUBLANES, lanes]
                        sel = jnp.where(p >= thr_row, p, 0.0)
                        gate[r] = sel if gate[r] is None else gate[r] + sel
                gate = jnp.concatenate(gate, axis=0)
                act_scr[k * n_keys:(k + 1) * n_keys, lanes] = (g * gate).astype(BF16)

    def contract(s):
        rows = slice(s * EXPERT_SUB, (s + 1) * EXPERT_SUB)
        acc_scr[...] += jnp.dot(vt_ref[:, rows], act_scr[rows, :], preferred_element_type=F32)

    for s in range(n_sub):
        pre_activation(s)
    for s in range(n_sub):
        activation(s)
        contract(s)

    @pl.when(j == pl.num_programs(1) - 1)
    def _():
        y = acc_scr[...].T
        x2 = x1_ref[...] + mod_ref[0, 5:6, :] * y
        ms = jnp.mean(x2 * x2, axis=-1, keepdims=True)
        o_ref[...] = x2 * lax.rsqrt(ms + NORM_EPS) * g_ref[...]


def _experts(h2t, e1, e2, thr, peer_u, peer_vt, x1, mod3, final_g, *, seq, tm, te):
    D, T = h2t.shape
    n_heads, n_keys, _ = e1.shape
    E = peer_u.shape[0]
    tiles_per_seq = seq // tm
    n_i1 = te // n_keys
    assert n_i1 == SUBLANES, "expert rows of one block fill the sublanes of an f32 tile"
    kern = functools.partial(_expert_kernel, tm=tm, te=te, n_heads=n_heads, n_keys=n_keys)
    row_blk = pl.BlockSpec((n_heads, n_i1, tm), lambda i, j: (0, j, i))
    key_blk = pl.BlockSpec((n_heads, n_keys, tm), lambda i, j: (0, 0, i))
    return pl.pallas_call(
        kern,
        grid=(T // tm, E // te),
        in_specs=[
            pl.BlockSpec((D, tm), lambda i, j: (0, i)),
            row_blk, key_blk,
            pl.BlockSpec((n_heads, tm), lambda i, j: (0, i)),
            pl.BlockSpec((te, D), lambda i, j: (j, 0)),
            pl.BlockSpec((None, D, te), lambda i, j: (j, 0, 0)),
            pl.BlockSpec((tm, D), lambda i, j: (i, 0)),
            pl.BlockSpec((1, 6, D), lambda i, j: (i // tiles_per_seq, 0, 0)),
            pl.BlockSpec((1, D), lambda i, j: (0, 0)),
        ],
        out_specs=pl.BlockSpec((tm, D), lambda i, j: (i, 0)),
        out_shape=jax.ShapeDtypeStruct((T, D), F32),
        scratch_shapes=[pltpu.VMEM((D, tm), F32),
                        pltpu.VMEM((te, tm), BF16),
                        pltpu.VMEM((te, tm), F32)],
        compiler_params=_params(("arbitrary", "arbitrary")),
        name="peer_experts",
    )(h2t, e1, e2, thr, peer_u, peer_vt, x1, mod3, final_g)


def kernel(x, c, positions, w_mod, b_mod, norm1_g, w_in, w_pool, pool_scale, w_out, norm2_g,
           w_query, sub_keys, peer_u, peer_v, final_g):
    B, S, D = x.shape
    assert w_mod.shape[0] == 1, "single-layer stack"
    assert S % ATTN_SPAN == 0
    tm = min(512, S)
    aw = w_in.shape[2] // 4
    x2d = x.reshape(B * S, D)
    pos_col = positions.reshape(B * S, 1)
    inv = ROPE_THETA ** (-jnp.arange(ROPE_HALF, dtype=F32) * 2.0 / ROPE_DIM)
    inv_row = jnp.zeros((1, LANES), F32).at[0, :ROPE_DIM].set(jnp.concatenate([inv, inv]))

    mod3 = _modulation(c, w_mod[0], b_mod[0]).reshape(B, 6, D)
    qkv, pool = _in_projection(x2d, mod3, norm1_g[0].reshape(1, D), pos_col, inv_row,
                               w_in[0].astype(BF16), w_pool[0].astype(BF16),
                               pool_scale[0].reshape(1, -1), seq=S, tm=tm)
    attn = _attention(qkv, batch=B, seq=S, attn_width=aw)
    x1, h2, h2t = _out_projection(attn, pool, x2d, mod3, norm2_g[0].reshape(1, D),
                                  w_out[0].astype(BF16), seq=S, tm=tm)
    e1, e2, thr = _routing(h2, w_query[0].astype(BF16), sub_keys[0].astype(BF16), tm=tm)
    te = 1024
    n_exp = peer_v.shape[1]
    vt_blocks = peer_v[0].reshape(n_exp // te, te, D).transpose(0, 2, 1).astype(BF16)
    out = _experts(h2t, e1, e2, thr, peer_u[0].astype(BF16), vt_blocks,
                   x1, mod3, final_g.reshape(1, D), seq=S, tm=tm, te=te)
    return out.reshape(B, S, D)
```

```python
import functools
import math

import jax
import jax.numpy as jnp
from jax import lax
from jax.experimental import pallas as pl
from jax.experimental.pallas import tpu as pltpu

F32 = jnp.float32
BF16 = jnp.bfloat16

HEAD_DIM = 128
ROPE_DIM = HEAD_DIM // 4
ROPE_HALF = ROPE_DIM // 2
ROPE_THETA = 500000.0
POOL_SIZES = (2, 4, 8, 16)
DILATIONS = (1, 4, 16)
ATTN_BLOCK = 128
TOPK = 16
NORM_EPS = 1e-6
NEG_BIG = -1e30

LANES = 128
SUBLANES = 8
VMEM_LIMIT = 56 * 1024 * 1024

ATTN_SPAN = ATTN_BLOCK * max(DILATIONS)
POOL_HALO = 16
EXPERT_SUB = 256


def _params(sem, vmem=VMEM_LIMIT):
    return pltpu.CompilerParams(dimension_semantics=sem, vmem_limit_bytes=vmem)


def _mod_kernel(c_ref, w_ref, b_ref, o_ref):
    c = c_ref[...]
    a = c * jax.nn.sigmoid(c)
    o_ref[...] = jnp.dot(a, w_ref[...], preferred_element_type=F32,
                         precision=lax.Precision.HIGHEST) + b_ref[...]


def _modulation(c, w_mod, b_mod):
    B, D = c.shape
    N = w_mod.shape[1]
    tn = 1024
    return pl.pallas_call(
        _mod_kernel,
        grid=(N // tn,),
        in_specs=[pl.BlockSpec((B, D), lambda j: (0, 0)),
                  pl.BlockSpec((D, tn), lambda j: (0, j)),
                  pl.BlockSpec((1, tn), lambda j: (0, j))],
        out_specs=pl.BlockSpec((B, tn), lambda j: (0, j)),
        out_shape=jax.ShapeDtypeStruct((B, N), F32),
        compiler_params=_params(("arbitrary",)),
        name="modulation",
    )(c, w_mod, b_mod.reshape(1, N))


def _rmsnorm_mod(x, g, shift, scale):
    ms = jnp.mean(x * x, axis=-1, keepdims=True)
    y = x * lax.rsqrt(ms + NORM_EPS) * g
    return y * (1.0 + scale) + shift


def _inproj_kernel(x_ref, mod_ref, g_ref, pos_ref, inv_ref, w_ref, wpool_ref, pscale_ref,
                   qkv_ref, pool_ref, h_scr, cos_scr, sina_scr, sinb_scr, carry_scr, ext_scr,
                   *, tm, tiles_per_seq, attn_width):
    i = pl.program_id(0)
    j = pl.program_id(1)
    n_heads = attn_width // HEAD_DIM

    @pl.when(j == 0)
    def _():
        h = _rmsnorm_mod(x_ref[...], g_ref[...], mod_ref[0, 0:1, :], mod_ref[0, 1:2, :])
        h_scr[...] = h.astype(BF16)
        ang = pos_ref[...].astype(F32) * inv_ref[...]
        lane = lax.broadcasted_iota(jnp.int32, ang.shape, 1)
        s = jnp.sin(ang)
        cos_scr[...] = jnp.cos(ang)
        sina_scr[...] = jnp.where(lane < ROPE_HALF, -s, 0.0)
        sinb_scr[...] = jnp.where((lane >= ROPE_HALF) & (lane < ROPE_DIM), s, 0.0)

    z = jnp.dot(h_scr[...], w_ref[...], preferred_element_type=F32)

    def rotary(scale):
        cos = cos_scr[...]
        sina = sina_scr[...]
        sinb = sinb_scr[...]
        for hh in range(n_heads):
            zh = z[:, hh * HEAD_DIM:(hh + 1) * HEAD_DIM]
            up = pltpu.roll(zh, HEAD_DIM - ROPE_HALF, axis=1)
            dn = pltpu.roll(zh, ROPE_HALF, axis=1)
            out = zh * cos + up * sina + dn * sinb
            if scale is not None:
                out = out * scale
            qkv_ref[:, hh * HEAD_DIM:(hh + 1) * HEAD_DIM] = out

    @pl.when(j == 0)
    def _():
        rotary(HEAD_DIM ** -0.5)

    @pl.when(j == 1)
    def _():
        rotary(None)

    @pl.when(j == 2)
    def _():
        qkv_ref[...] = z

    @pl.when(j == 3)
    def _():
        first = (i % tiles_per_seq) == 0

        @pl.when(first)
        def _():
            carry_scr[...] = jnp.zeros_like(carry_scr)

        ext_scr[0:POOL_HALO, :] = carry_scr[...]
        ext_scr[POOL_HALO:POOL_HALO + tm, :] = z
        carry_scr[...] = z[tm - POOL_HALO:tm, :]
        gw = z.shape[1] // len(POOL_SIZES)
        t_in_seq = (i % tiles_per_seq) * tm + lax.broadcasted_iota(jnp.int32, (tm, gw), 0)
        for gi, p in enumerate(POOL_SIZES):
            cols = slice(gi * gw, (gi + 1) * gw)
            u_g = z[:, cols]
            acc = u_g
            for back in range(1, p):
                acc = acc + ext_scr[POOL_HALO - back:POOL_HALO - back + tm, cols]
            cnt = jnp.minimum(t_in_seq + 1, p).astype(F32)
            r = acc / cnt - u_g
            y = jnp.dot(r.astype(BF16), wpool_ref[gi], preferred_element_type=F32)
            pool_ref[:, cols] = (y * pscale_ref[:, cols]).astype(pool_ref.dtype)


def _in_projection(x2d, mod3, norm_g, pos_col, inv_row, w_in, w_pool, pool_scale, *, seq, tm):
    T, D = x2d.shape
    n_in = w_in.shape[1]
    aw = n_in // 4
    tiles_per_seq = seq // tm
    kern = functools.partial(_inproj_kernel, tm=tm, tiles_per_seq=tiles_per_seq, attn_width=aw)
    return pl.pallas_call(
        kern,
        grid=(T // tm, 4),
        in_specs=[
            pl.BlockSpec((tm, D), lambda i, j: (i, 0)),
            pl.BlockSpec((1, 6, D), lambda i, j: (i // tiles_per_seq, 0, 0)),
            pl.BlockSpec((1, D), lambda i, j: (0, 0)),
            pl.BlockSpec((tm, 1), lambda i, j: (i, 0)),
            pl.BlockSpec((1, LANES), lambda i, j: (0, 0)),
            pl.BlockSpec((D, aw), lambda i, j: (0, j)),
            pl.BlockSpec(w_pool.shape, lambda i, j: (0, 0, 0)),
            pl.BlockSpec((1, aw), lambda i, j: (0, 0)),
        ],
        out_specs=[
            pl.BlockSpec((tm, aw), lambda i, j: (i, jnp.minimum(j, 2))),
            pl.BlockSpec((tm, aw), lambda i, j: (i, 0)),
        ],
        out_shape=[jax.ShapeDtypeStruct((T, 3 * aw), F32),
                   jax.ShapeDtypeStruct((T, aw), BF16)],
        scratch_shapes=[
            pltpu.VMEM((tm, D), BF16),
            pltpu.VMEM((tm, LANES), F32),
            pltpu.VMEM((tm, LANES), F32),
            pltpu.VMEM((tm, LANES), F32),
            pltpu.VMEM((POOL_HALO, aw), F32),
            pltpu.VMEM((POOL_HALO + tm, aw), F32),
        ],
        compiler_params=_params(("arbitrary", "arbitrary")),
        name="in_projection",
    )(x2d, mod3, norm_g, pos_col, inv_row, w_in, w_pool, pool_scale)


def _strided(start, size, stride):
    return pl.ds(start, size) if stride == 1 else pl.ds(start, size, stride=stride)


def _attn_kernel(q_ref, kp_ref, kc_ref, vp_ref, vc_ref, o_ref, o_scr, l_scr):
    n = pl.program_id(2)
    blk = ATTN_BLOCK
    qi = lax.broadcasted_iota(jnp.int32, (blk, 2 * blk), 0)
    kj = lax.broadcasted_iota(jnp.int32, (blk, 2 * blk), 1)
    dist = qi + blk - kj
    band = (dist >= 0) & (dist <= blk)
    band_first = band & ((kj >= blk) | (n > 0))

    for pi, d in enumerate(DILATIONS):
        for r in range(d):
            for c in range(ATTN_SPAN // (blk * d)):
                start = r + d * blk * c
                rows = _strided(start, blk, d)
                q = q_ref[rows, :].astype(BF16)
                if c == 0:
                    lo_rows = _strided(ATTN_SPAN + r - d * blk, blk, d)
                    k_lo, v_lo = kp_ref[lo_rows, :], vp_ref[lo_rows, :]
                else:
                    lo_rows = _strided(start - d * blk, blk, d)
                    k_lo, v_lo = kc_ref[lo_rows, :], vc_ref[lo_rows, :]
                k = jnp.concatenate([k_lo, kc_ref[rows, :]], axis=0).astype(BF16)
                v = jnp.concatenate([v_lo, vc_ref[rows, :]], axis=0).astype(BF16)
                s = lax.dot_general(q, k, (((1,), (1,)), ((), ())), preferred_element_type=F32)
                s = jnp.where(band_first if c == 0 else band, s, NEG_BIG)
                m = jnp.max(s, axis=-1, keepdims=True)
                p = jnp.exp(s - m)
                den = jnp.sum(p, axis=-1, keepdims=True)
                o = jnp.dot(p.astype(BF16), v, preferred_element_type=F32) / den
                o_scr[pi, rows, :] = o
                l_scr[pi, rows, :] = jnp.broadcast_to(m + jnp.log(den), (blk, HEAD_DIM))

    l0, l1, l2 = l_scr[0], l_scr[1], l_scr[2]
    lmax = jnp.maximum(jnp.maximum(l0, l1), l2)
    e0, e1, e2 = jnp.exp(l0 - lmax), jnp.exp(l1 - lmax), jnp.exp(l2 - lmax)
    mixed = (e0 * o_scr[0] + e1 * o_scr[1] + e2 * o_scr[2]) / (e0 + e1 + e2)
    o_ref[...] = mixed.astype(o_ref.dtype)


def _attention(qkv, *, batch, seq, attn_width):
    T = qkv.shape[0]
    n_heads = attn_width // HEAD_DIM
    spans = seq // ATTN_SPAN
    blk = (ATTN_SPAN, HEAD_DIM)

    def cur(col0):
        return lambda b, h, n: (b * spans + n, col0 + h)

    def prev(col0):
        return lambda b, h, n: (b * spans + jnp.maximum(n - 1, 0), col0 + h)

    return pl.pallas_call(
        _attn_kernel,
        grid=(batch, n_heads, spans),
        in_specs=[pl.BlockSpec(blk, cur(0)),
                  pl.BlockSpec(blk, prev(n_heads)),
                  pl.BlockSpec(blk, cur(n_heads)),
                  pl.BlockSpec(blk, prev(2 * n_heads)),
                  pl.BlockSpec(blk, cur(2 * n_heads))],
        out_specs=pl.BlockSpec(blk, cur(0)),
        out_shape=jax.ShapeDtypeStruct((T, attn_width), BF16),
        scratch_shapes=[pltpu.VMEM((len(DILATIONS),) + blk, F32),
                        pltpu.VMEM((len(DILATIONS),) + blk, F32)],
        compiler_params=_params(("arbitrary", "arbitrary", "arbitrary")),
        name="dilated_attention",
    )(qkv, qkv, qkv, qkv, qkv)


def _outproj_kernel(attn_ref, pool_ref, x_ref, mod_ref, g_ref, w_ref, x1_ref, h2_ref, h2t_ref,
                    *, attn_width):
    mix = jnp.dot(attn_ref[...], w_ref[0:attn_width, :], preferred_element_type=F32)
    mix = mix + jnp.dot(pool_ref[...], w_ref[attn_width:, :], preferred_element_type=F32)
    x1 = x_ref[...] + mod_ref[0, 2:3, :] * mix
    x1_ref[...] = x1
    h2 = _rmsnorm_mod(x1, g_ref[...], mod_ref[0, 3:4, :], mod_ref[0, 4:5, :])
    h2_ref[...] = h2.astype(BF16)
    h2t_ref[...] = h2.T.astype(BF16)


def _out_projection(attn, pool, x2d, mod3, norm_g, w_out, *, seq, tm):
    T, D = x2d.shape
    aw = attn.shape[1]
    pw = pool.shape[1]
    tiles_per_seq = seq // tm
    return pl.pallas_call(
        functools.partial(_outproj_kernel, attn_width=aw),
        grid=(T // tm,),
        in_specs=[
            pl.BlockSpec((tm, aw), lambda i: (i, 0)),
            pl.BlockSpec((tm, pw), lambda i: (i, 0)),
            pl.BlockSpec((tm, D), lambda i: (i, 0)),
            pl.BlockSpec((1, 6, D), lambda i: (i // tiles_per_seq, 0, 0)),
            pl.BlockSpec((1, D), lambda i: (0, 0)),
            pl.BlockSpec((aw + pw, D), lambda i: (0, 0)),
        ],
        out_specs=[pl.BlockSpec((tm, D), lambda i: (i, 0)),
                   pl.BlockSpec((tm, D), lambda i: (i, 0)),
                   pl.BlockSpec((D, tm), lambda i: (0, i))],
        out_shape=[jax.ShapeDtypeStruct((T, D), F32),
                   jax.ShapeDtypeStruct((T, D), BF16),
                   jax.ShapeDtypeStruct((D, T), BF16)],
        compiler_params=_params(("arbitrary",)),
        name="out_projection",
    )(attn, pool, x2d, mod3, norm_g, w_out)


def _oddeven_merge_sort_pairs(n):
    pairs = []

    def merge(lo, hi, r):
        step = r * 2
        if step < hi - lo:
            merge(lo, hi, step)
            merge(lo + r, hi, step)
            for k in range(lo + r, hi - r, step):
                pairs.append((k, k + r))
        else:
            pairs.append((lo, lo + r))

    def sort(lo, hi):
        if hi - lo >= 1:
            mid = lo + (hi - lo) // 2
            sort(lo, mid)
            sort(mid + 1, hi)
            merge(lo, hi, 1)

    sort(0, n - 1)
    return pairs


_SORT16 = _oddeven_merge_sort_pairs(TOPK)


def _sort_desc(vals):
    vals = list(vals)
    for a, b in _SORT16:
        hi, lo = jnp.maximum(vals[a], vals[b]), jnp.minimum(vals[a], vals[b])
        vals[a], vals[b] = hi, lo
    return vals


def _merge_top(a_list, b_list):
    n = TOPK
    c = [jnp.maximum(a_list[k], b_list[n - 1 - k]) for k in range(n)]
    stride = n // 2
    while stride >= 1:
        for k in range(n):
            if (k & stride) == 0:
                hi, lo = jnp.maximum(c[k], c[k + stride]), jnp.minimum(c[k], c[k + stride])
                c[k], c[k + stride] = hi, lo
        stride //= 2
    return c


_CAND_ROWS = [[(i, j) for j in range(TOPK) if (i + 1) * (j + 1) <= TOPK] for i in range(TOPK)]


def _route_kernel(h_ref, wq_ref, keys_ref, e1_ref, e2_ref, thr_ref, qp_scr, top_scr,
                  *, tm, n_heads, n_keys):
    qp = jnp.dot(h_ref[...], wq_ref[...], preferred_element_type=F32)
    qp_scr[...] = qp.astype(BF16)
    n_chunks = tm // LANES
    groups = n_keys // SUBLANES
    assert groups == TOPK

    for h in range(n_heads):
        for half, out_ref in ((0, e1_ref), (1, e2_ref)):
            col0 = (2 * h + half) * n_keys
            s_t = lax.dot_general(keys_ref[h, half], qp_scr[:, col0:col0 + n_keys],
                                  (((1,), (1,)), ((), ())), preferred_element_type=F32)
            out_ref[h] = s_t
            for cidx in range(n_chunks):
                lanes = slice(cidx * LANES, (cidx + 1) * LANES)
                blk = s_t[:, lanes]
                vals = _sort_desc([blk[g * SUBLANES:(g + 1) * SUBLANES, :] for g in range(groups)])
                for shift in (4, 2, 1):
                    partner = [pltpu.roll(v, shift, axis=0) for v in vals]
                    vals = _merge_top(vals, partner)
                for k in range(TOPK):
                    top_scr[half, k, h:h + 1, lanes] = vals[k][0:1, :]

    for cidx in range(n_chunks):
        lanes = slice(cidx * LANES, (cidx + 1) * LANES)
        a = [top_scr[0, k, :, lanes] for k in range(TOPK)]
        b = [top_scr[1, k, :, lanes] for k in range(TOPK)]
        ea = [jnp.exp(v - a[0]) for v in a]
        eb = [jnp.exp(v - b[0]) for v in b]
        pad = jnp.full_like(a[0], -1.0)
        best = None
        for row in _CAND_ROWS:
            lst = [ea[i] * eb[j] for (i, j) in row]
            lst = lst + [pad] * (TOPK - len(lst))
            best = lst if best is None else _merge_top(best, lst)
        z = best[0]
        for k in range(1, TOPK):
            z = z + best[k]
        rz = 1.0 / z
        cut = best[TOPK - 1]
        ebn = [v * rz for v in eb]
        thr = None
        for row in _CAND_ROWS:
            for (i, j) in row:
                sel = ea[i] * eb[j] >= cut
                cand = jnp.where(sel, ea[i] * ebn[j], jnp.inf)
                thr = cand if thr is None else jnp.minimum(thr, cand)
        thr_ref[:, lanes] = thr
        top_scr[0, 0, :, lanes] = a[0]
        top_scr[1, 0, :, lanes] = b[0]
        top_scr[1, 1, :, lanes] = rz

    for h in range(n_heads):
        m1 = top_scr[0, 0, h:h + 1, :]
        m2 = top_scr[1, 0, h:h + 1, :]
        rz = top_scr[1, 1, h:h + 1, :]
        e1_ref[h] = jnp.exp(e1_ref[h] - m1)
        e2_ref[h] = jnp.exp(e2_ref[h] - m2) * rz


def _routing(h2, w_query, sub_keys, *, tm):
    T, D = h2.shape
    n_heads, _, n_keys, kd = sub_keys.shape
    qw = w_query.shape[1]
    kern = functools.partial(_route_kernel, tm=tm, n_heads=n_heads, n_keys=n_keys)
    key_blk = pl.BlockSpec((n_heads, n_keys, tm), lambda i: (0, 0, i))
    return pl.pallas_call(
        kern,
        grid=(T // tm,),
        in_specs=[pl.BlockSpec((tm, D), lambda i: (i, 0)),
                  pl.BlockSpec((D, qw), lambda i: (0, 0)),
                  pl.BlockSpec(sub_keys.shape, lambda i: (0, 0, 0, 0))],
        out_specs=[key_blk, key_blk, pl.BlockSpec((n_heads, tm), lambda i: (0, i))],
        out_shape=[jax.ShapeDtypeStruct((n_heads, n_keys, T), F32),
                   jax.ShapeDtypeStruct((n_heads, n_keys, T), F32),
                   jax.ShapeDtypeStruct((n_heads, T), F32)],
        scratch_shapes=[pltpu.VMEM((tm, qw), BF16),
                        pltpu.VMEM((2, TOPK, n_heads, tm), F32)],
        compiler_params=_params(("arbitrary",)),
        name="peer_routing",
    )(h2, w_query, sub_keys)


def _gelu(a):
    return a * (lax.erf(a * (1.0 / math.sqrt(2.0))) + 1.0) * 0.5


def _expert_kernel(ht_ref, e1_ref, e2_ref, thr_ref, u_ref, vt_ref, x1_ref, mod_ref, g_ref,
                   o_ref, acc_scr, act_scr, pre_scr, *, tm, te, n_heads, n_keys):
    j = pl.program_id(1)
    n_i1 = te // n_keys
    n_chunks = tm // LANES

    @pl.when(j == 0)
    def _():
        acc_scr[...] = jnp.zeros_like(acc_scr)

    n_sub = te // EXPERT_SUB
    i1_per_sub = EXPERT_SUB // n_keys
    n_groups = n_keys // SUBLANES

    def pre_activation(s):
        rows = slice(s * EXPERT_SUB, (s + 1) * EXPERT_SUB)
        pre_scr[rows, :] = jnp.dot(u_ref[rows, :], ht_ref[...], preferred_element_type=F32)

    def activation(s):
        for k in range(s * i1_per_sub, (s + 1) * i1_per_sub):
            for cidx in range(n_chunks):
                lanes = slice(cidx * LANES, (cidx + 1) * LANES)
                g = _gelu(pre_scr[k * n_keys:(k + 1) * n_keys, lanes])
                tie = g[0:SUBLANES, :] * 0.0
                gate = [None] * n_groups
                for h in range(n_heads):
                    e1_rep = e1_ref[h, k:k + 1, lanes] + tie
                    thr_row = thr_ref[h:h + 1, lanes]
                    for r in range(n_groups):
                        p = e1_rep * e2_ref[h, r * SUBLANES:(r + 1) * SUBLANES, lanes]
                        sel = jnp.where(p >= thr_row, p, 0.0)
                        gate[r] = sel if gate[r] is None else gate[r] + sel
                gate = jnp.concatenate(gate, axis=0)
                act_scr[k * n_keys:(k + 1) * n_keys, lanes] = (g * gate).astype(BF16)

    def contract(s):
        rows = slice(s * EXPERT_SUB, (s + 1) * EXPERT_SUB)
        acc_scr[...] += jnp.dot(vt_ref[:, rows], act_scr[rows, :], preferred_element_type=F32)

    for s in range(n_sub):
        pre_activation(s)
    for s in range(n_sub):
        activation(s)
        contract(s)

    @pl.when(j == pl.num_programs(1) - 1)
    def _():
        y = acc_scr[...].T
        x2 = x1_ref[...] + mod_ref[0, 5:6, :] * y
        ms = jnp.mean(x2 * x2, axis=-1, keepdims=True)
        o_ref[...] = x2 * lax.rsqrt(ms + NORM_EPS) * g_ref[...]


def _experts(h2t, e1, e2, thr, peer_u, peer_vt, x1, mod3, final_g, *, seq, tm, te):
    D, T = h2t.shape
    n_heads, n_keys, _ = e1.shape
    E = peer_u.shape[0]
    tiles_per_seq = seq // tm
    n_i1 = te // n_keys
    assert n_i1 == SUBLANES, "expert rows of one block fill the sublanes of an f32 tile"
    kern = functools.partial(_expert_kernel, tm=tm, te=te, n_heads=n_heads, n_keys=n_keys)
    row_blk = pl.BlockSpec((n_heads, n_i1, tm), lambda i, j: (0, j, i))
    key_blk = pl.BlockSpec((n_heads, n_keys, tm), lambda i, j: (0, 0, i))
    return pl.pallas_call(
        kern,
        grid=(T // tm, E // te),
        in_specs=[
            pl.BlockSpec((D, tm), lambda i, j: (0, i)),
            row_blk, key_blk,
            pl.BlockSpec((n_heads, tm), lambda i, j: (0, i)),
            pl.BlockSpec((te, D), lambda i, j: (j, 0)),
            pl.BlockSpec((None, D, te), lambda i, j: (j, 0, 0)),
            pl.BlockSpec((tm, D), lambda i, j: (i, 0)),
            pl.BlockSpec((1, 6, D), lambda i, j: (i // tiles_per_seq, 0, 0)),
            pl.BlockSpec((1, D), lambda i, j: (0, 0)),
        ],
        out_specs=pl.BlockSpec((tm, D), lambda i, j: (i, 0)),
        out_shape=jax.ShapeDtypeStruct((T, D), F32),
        scratch_shapes=[pltpu.VMEM((D, tm), F32),
                        pltpu.VMEM((te, tm), BF16),
                        pltpu.VMEM((te, tm), F32)],
        compiler_params=_params(("arbitrary", "arbitrary")),
        name="peer_experts",
    )(h2t, e1, e2, thr, peer_u, peer_vt, x1, mod3, final_g)


def kernel(x, c, positions, w_mod, b_mod, norm1_g, w_in, w_pool, pool_scale, w_out, norm2_g,
           w_query, sub_keys, peer_u, peer_v, final_g):
    B, S, D = x.shape
    assert w_mod.shape[0] == 1, "single-layer stack"
    assert S % ATTN_SPAN == 0
    tm = min(512, S)
    aw = w_in.shape[2] // 4
    x2d = x.reshape(B * S, D)
    pos_col = positions.reshape(B * S, 1)
    inv = ROPE_THETA ** (-jnp.arange(ROPE_HALF, dtype=F32) * 2.0 / ROPE_DIM)
    inv_row = jnp.zeros((1, LANES), F32).at[0, :ROPE_DIM].set(jnp.concatenate([inv, inv]))

    mod3 = _modulation(c, w_mod[0], b_mod[0]).reshape(B, 6, D)
    qkv, pool = _in_projection(x2d, mod3, norm1_g[0].reshape(1, D), pos_col, inv_row,
                               w_in[0].astype(BF16), w_pool[0].astype(BF16),
                               pool_scale[0].reshape(1, -1), seq=S, tm=tm)
    attn = _attention(qkv, batch=B, seq=S, attn_width=aw)
    x1, h2, h2t = _out_projection(attn, pool, x2d, mod3, norm2_g[0].reshape(1, D),
                                  w_out[0].astype(BF16), seq=S, tm=tm)
    e1, e2, thr = _routing(h2, w_query[0].astype(BF16), sub_keys[0].astype(BF16), tm=tm)
    te = 1024
    n_exp = peer_v.shape[1]
    vt_blocks = peer_v[0].reshape(n_exp // te, te, D).transpose(0, 2, 1).astype(BF16)
    out = _experts(h2t, e1, e2, thr, peer_u[0].astype(BF16), vt_blocks,
                   x1, mod3, final_g.reshape(1, D), seq=S, tm=tm // 2, te=te)
    return out.reshape(B, S, D)
```

```python
import functools
import math

import jax
import jax.numpy as jnp
from jax import lax
from jax.experimental import pallas as pl
from jax.experimental.pallas import tpu as pltpu

F32 = jnp.float32
BF16 = jnp.bfloat16

HEAD_DIM = 128
ROPE_DIM = HEAD_DIM // 4
ROPE_HALF = ROPE_DIM // 2
ROPE_THETA = 500000.0
POOL_SIZES = (2, 4, 8, 16)
DILATIONS = (1, 4, 16)
ATTN_BLOCK = 128
TOPK = 16
NORM_EPS = 1e-6
NEG_BIG = -1e30

LANES = 128
SUBLANES = 8
VMEM_LIMIT = 56 * 1024 * 1024

ATTN_SPAN = ATTN_BLOCK * max(DILATIONS)
POOL_HALO = 16
EXPERT_SUB = 256


def _params(sem, vmem=VMEM_LIMIT):
    return pltpu.CompilerParams(dimension_semantics=sem, vmem_limit_bytes=vmem)


def _mod_kernel(c_ref, w_ref, b_ref, o_ref):
    c = c_ref[...]
    a = c * jax.nn.sigmoid(c)
    o_ref[...] = jnp.dot(a, w_ref[...], preferred_element_type=F32,
                         precision=lax.Precision.HIGHEST) + b_ref[...]


def _modulation(c, w_mod, b_mod):
    B, D = c.shape
    N = w_mod.shape[1]
    tn = 1024
    return pl.pallas_call(
        _mod_kernel,
        grid=(N // tn,),
        in_specs=[pl.BlockSpec((B, D), lambda j: (0, 0)),
                  pl.BlockSpec((D, tn), lambda j: (0, j)),
                  pl.BlockSpec((1, tn), lambda j: (0, j))],
        out_specs=pl.BlockSpec((B, tn), lambda j: (0, j)),
        out_shape=jax.ShapeDtypeStruct((B, N), F32),
        compiler_params=_params(("arbitrary",)),
        name="modulation",
    )(c, w_mod, b_mod.reshape(1, N))


def _rmsnorm_mod(x, g, shift, scale):
    ms = jnp.mean(x * x, axis=-1, keepdims=True)
    y = x * lax.rsqrt(ms + NORM_EPS) * g
    return y * (1.0 + scale) + shift


def _rope_kernel(pos_ref, inv_ref, cos_ref, sin_ref):
    ang = pos_ref[...].astype(F32) * inv_ref[...]
    cos_ref[...] = jnp.cos(ang)
    sin_ref[...] = jnp.sin(ang)


def _rope_tables(positions):
    T = positions.size
    per_row = LANES // ROPE_HALF
    pos_rep = jnp.repeat(positions.reshape(T // per_row, per_row), ROPE_HALF, axis=1)
    inv = ROPE_THETA ** (-jnp.arange(ROPE_HALF, dtype=F32) * 2.0 / ROPE_DIM)
    inv_rep = jnp.tile(inv, per_row).reshape(1, LANES)
    dense = pl.BlockSpec((T // per_row, LANES), lambda: (0, 0))
    cos_c, sin_c = pl.pallas_call(
        _rope_kernel,
        in_specs=[dense, pl.BlockSpec((1, LANES), lambda: (0, 0))],
        out_specs=[dense, dense],
        out_shape=[jax.ShapeDtypeStruct((T // per_row, LANES), F32)] * 2,
        name="rope_tables",
    )(pos_rep, inv_rep)
    cos_h, sin_h = cos_c.reshape(T, ROPE_HALF), sin_c.reshape(T, ROPE_HALF)
    zero_h = jnp.zeros_like(sin_h)
    rest = LANES - ROPE_DIM
    cos = jnp.concatenate([cos_h, cos_h, jnp.ones((T, rest), F32)], axis=1)
    sin_lo = jnp.concatenate([sin_h, zero_h, jnp.zeros((T, rest), F32)], axis=1)
    sin_hi = jnp.concatenate([zero_h, sin_h, jnp.zeros((T, rest), F32)], axis=1)
    return cos, sin_lo, sin_hi


def _inproj_kernel(x_ref, mod_ref, g_ref, cos_ref, sinlo_ref, sinhi_ref, w_ref, wpool_ref, pscale_ref,
                   qkv_ref, pool_ref, h_scr, carry_scr, ext_scr,
                   *, tm, tiles_per_seq, attn_width):
    i = pl.program_id(0)
    j = pl.program_id(1)
    n_heads = attn_width // HEAD_DIM

    @pl.when(j == 0)
    def _():
        h = _rmsnorm_mod(x_ref[...], g_ref[...], mod_ref[0, 0:1, :], mod_ref[0, 1:2, :])
        h_scr[...] = h.astype(BF16)

    z = jnp.dot(h_scr[...], w_ref[...], preferred_element_type=F32)

    def rotary(scale):
        cos = cos_ref[...]
        sin_lo = sinlo_ref[...]
        sin_hi = sinhi_ref[...]
        for hh in range(n_heads):
            zh = z[:, hh * HEAD_DIM:(hh + 1) * HEAD_DIM]
            up = pltpu.roll(zh, HEAD_DIM - ROPE_HALF, axis=1)
            dn = pltpu.roll(zh, ROPE_HALF, axis=1)
            out = zh * cos - up * sin_lo + dn * sin_hi
            if scale is not None:
                out = out * scale
            qkv_ref[:, hh * HEAD_DIM:(hh + 1) * HEAD_DIM] = out

    @pl.when(j == 0)
    def _():
        rotary(HEAD_DIM ** -0.5)

    @pl.when(j == 1)
    def _():
        rotary(None)

    @pl.when(j == 2)
    def _():
        qkv_ref[...] = z

    @pl.when(j == 3)
    def _():
        first = (i % tiles_per_seq) == 0

        @pl.when(first)
        def _():
            carry_scr[...] = jnp.zeros_like(carry_scr)

        ext_scr[0:POOL_HALO, :] = carry_scr[...]
        ext_scr[POOL_HALO:POOL_HALO + tm, :] = z
        carry_scr[...] = z[tm - POOL_HALO:tm, :]
        gw = z.shape[1] // len(POOL_SIZES)
        t_in_seq = (i % tiles_per_seq) * tm + lax.broadcasted_iota(jnp.int32, (tm, gw), 0)
        for gi, p in enumerate(POOL_SIZES):
            cols = slice(gi * gw, (gi + 1) * gw)
            u_g = z[:, cols]
            acc = u_g
            for back in range(1, p):
                acc = acc + ext_scr[POOL_HALO - back:POOL_HALO - back + tm, cols]
            cnt = jnp.minimum(t_in_seq + 1, p).astype(F32)
            r = acc / cnt - u_g
            y = jnp.dot(r.astype(BF16), wpool_ref[gi], preferred_element_type=F32)
            pool_ref[:, cols] = (y * pscale_ref[:, cols]).astype(pool_ref.dtype)


def _in_projection(x2d, mod3, norm_g, rope, w_in, w_pool, pool_scale, *, seq, tm):
    T, D = x2d.shape
    n_in = w_in.shape[1]
    aw = n_in // 4
    tiles_per_seq = seq // tm
    kern = functools.partial(_inproj_kernel, tm=tm, tiles_per_seq=tiles_per_seq, attn_width=aw)
    return pl.pallas_call(
        kern,
        grid=(T // tm, 4),
        in_specs=[
            pl.BlockSpec((tm, D), lambda i, j: (i, 0)),
            pl.BlockSpec((1, 6, D), lambda i, j: (i // tiles_per_seq, 0, 0)),
            pl.BlockSpec((1, D), lambda i, j: (0, 0)),
            pl.BlockSpec((tm, LANES), lambda i, j: (i, 0)),
            pl.BlockSpec((tm, LANES), lambda i, j: (i, 0)),
            pl.BlockSpec((tm, LANES), lambda i, j: (i, 0)),
            pl.BlockSpec((D, aw), lambda i, j: (0, j)),
            pl.BlockSpec(w_pool.shape, lambda i, j: (0, 0, 0)),
            pl.BlockSpec((1, aw), lambda i, j: (0, 0)),
        ],
        out_specs=[
            pl.BlockSpec((tm, aw), lambda i, j: (i, jnp.minimum(j, 2))),
            pl.BlockSpec((tm, aw), lambda i, j: (i, 0)),
        ],
        out_shape=[jax.ShapeDtypeStruct((T, 3 * aw), F32),
                   jax.ShapeDtypeStruct((T, aw), BF16)],
        scratch_shapes=[
            pltpu.VMEM((tm, D), BF16),
            pltpu.VMEM((POOL_HALO, aw), F32),
            pltpu.VMEM((POOL_HALO + tm, aw), F32),
        ],
        compiler_params=_params(("arbitrary", "arbitrary")),
        name="in_projection",
    )(x2d, mod3, norm_g, *rope, w_in, w_pool, pool_scale)


def _strided(start, size, stride):
    return pl.ds(start, size) if stride == 1 else pl.ds(start, size, stride=stride)


def _attn_kernel(q_ref, kp_ref, kc_ref, vp_ref, vc_ref, o_ref, o_scr, l_scr, plane_scr, res_scr):
    n = pl.program_id(2)
    blk = ATTN_BLOCK
    qi = lax.broadcasted_iota(jnp.int32, (blk, 2 * blk), 0)
    kj = lax.broadcasted_iota(jnp.int32, (blk, 2 * blk), 1)
    dist = qi + blk - kj
    band = (dist >= 0) & (dist <= blk)
    band_first = band & ((kj >= blk) | (n > 0))

    def block(q, k_lo, k_hi, v_lo, v_hi, mask):
        k = jnp.concatenate([k_lo, k_hi], axis=0).astype(BF16)
        v = jnp.concatenate([v_lo, v_hi], axis=0).astype(BF16)
        s = lax.dot_general(q.astype(BF16), k, (((1,), (1,)), ((), ())),
                            preferred_element_type=F32)
        s = jnp.where(mask, s, NEG_BIG)
        m = jnp.max(s, axis=-1, keepdims=True)
        p = jnp.exp(s - m)
        den = jnp.sum(p, axis=-1, keepdims=True)
        o = jnp.dot(p.astype(BF16), v, preferred_element_type=F32) / den
        return o, jnp.broadcast_to(m + jnp.log(den), (blk, HEAD_DIM))

    for pi, d in enumerate(DILATIONS[:2]):
        for r in range(d):
            for c in range(ATTN_SPAN // (blk * d)):
                start = r + d * blk * c
                rows = _strided(start, blk, d)
                if c == 0:
                    lo_rows = _strided(ATTN_SPAN + r - d * blk, blk, d)
                    k_lo, v_lo = kp_ref[lo_rows, :], vp_ref[lo_rows, :]
                else:
                    lo_rows = _strided(start - d * blk, blk, d)
                    k_lo, v_lo = kc_ref[lo_rows, :], vc_ref[lo_rows, :]
                o, l = block(q_ref[rows, :], k_lo, kc_ref[rows, :], v_lo, vc_ref[rows, :],
                             band_first if c == 0 else band)
                o_scr[pi, rows, :] = o
                l_scr[pi, rows, :] = l

    sub, per_plane = 4, ATTN_SPAN // 4
    assert DILATIONS[2] == sub * sub and ATTN_SPAN == blk * DILATIONS[2]
    for rho in range(sub):
        plane = pl.ds(rho, per_plane, stride=sub)
        for t, src in enumerate((q_ref, kp_ref, kc_ref, vp_ref, vc_ref)):
            plane_scr[t] = src[plane, :]
        for c2 in range(sub):
            rows = pl.ds(c2, blk, stride=sub)
            o, l = block(plane_scr[0, rows, :], plane_scr[1, rows, :], plane_scr[2, rows, :],
                         plane_scr[3, rows, :], plane_scr[4, rows, :], band_first)
            res_scr[0, rows, :] = o
            res_scr[1, rows, :] = l
        o_scr[2, plane, :] = res_scr[0]
        l_scr[2, plane, :] = res_scr[1]

    l0, l1, l2 = l_scr[0], l_scr[1], l_scr[2]
    lmax = jnp.maximum(jnp.maximum(l0, l1), l2)
    e0, e1, e2 = jnp.exp(l0 - lmax), jnp.exp(l1 - lmax), jnp.exp(l2 - lmax)
    mixed = (e0 * o_scr[0] + e1 * o_scr[1] + e2 * o_scr[2]) / (e0 + e1 + e2)
    o_ref[...] = mixed.astype(o_ref.dtype)


def _attention(qkv, *, batch, seq, attn_width):
    T = qkv.shape[0]
    n_heads = attn_width // HEAD_DIM
    spans = seq // ATTN_SPAN
    blk = (ATTN_SPAN, HEAD_DIM)

    def cur(col0):
        return lambda b, h, n: (b * spans + n, col0 + h)

    def prev(col0):
        return lambda b, h, n: (b * spans + jnp.maximum(n - 1, 0), col0 + h)

    return pl.pallas_call(
        _attn_kernel,
        grid=(batch, n_heads, spans),
        in_specs=[pl.BlockSpec(blk, cur(0)),
                  pl.BlockSpec(blk, prev(n_heads)),
                  pl.BlockSpec(blk, cur(n_heads)),
                  pl.BlockSpec(blk, prev(2 * n_heads)),
                  pl.BlockSpec(blk, cur(2 * n_heads))],
        out_specs=pl.BlockSpec(blk, cur(0)),
        out_shape=jax.ShapeDtypeStruct((T, attn_width), BF16),
        scratch_shapes=[pltpu.VMEM((len(DILATIONS),) + blk, F32),
                        pltpu.VMEM((len(DILATIONS),) + blk, F32),
                        pltpu.VMEM((5, ATTN_SPAN // 4, HEAD_DIM), F32),
                        pltpu.VMEM((2, ATTN_SPAN // 4, HEAD_DIM), F32)],
        compiler_params=_params(("arbitrary", "arbitrary", "arbitrary")),
        name="dilated_attention",
    )(qkv, qkv, qkv, qkv, qkv)


def _outproj_kernel(attn_ref, pool_ref, x_ref, mod_ref, g_ref, w_ref, x1_ref, h2_ref, h2t_ref,
                    *, attn_width):
    mix = jnp.dot(attn_ref[...], w_ref[0:attn_width, :], preferred_element_type=F32)
    mix = mix + jnp.dot(pool_ref[...], w_ref[attn_width:, :], preferred_element_type=F32)
    x1 = x_ref[...] + mod_ref[0, 2:3, :] * mix
    x1_ref[...] = x1
    h2 = _rmsnorm_mod(x1, g_ref[...], mod_ref[0, 3:4, :], mod_ref[0, 4:5, :])
    h2_ref[...] = h2.astype(BF16)
    h2t_ref[...] = h2.T.astype(BF16)


def _out_projection(attn, pool, x2d, mod3, norm_g, w_out, *, seq, tm):
    T, D = x2d.shape
    aw = attn.shape[1]
    pw = pool.shape[1]
    tiles_per_seq = seq // tm
    return pl.pallas_call(
        functools.partial(_outproj_kernel, attn_width=aw),
        grid=(T // tm,),
        in_specs=[
            pl.BlockSpec((tm, aw), lambda i: (i, 0)),
            pl.BlockSpec((tm, pw), lambda i: (i, 0)),
            pl.BlockSpec((tm, D), lambda i: (i, 0)),
            pl.BlockSpec((1, 6, D), lambda i: (i // tiles_per_seq, 0, 0)),
            pl.BlockSpec((1, D), lambda i: (0, 0)),
            pl.BlockSpec((aw + pw, D), lambda i: (0, 0)),
        ],
        out_specs=[pl.BlockSpec((tm, D), lambda i: (i, 0)),
                   pl.BlockSpec((tm, D), lambda i: (i, 0)),
                   pl.BlockSpec((D, tm), lambda i: (0, i))],
        out_shape=[jax.ShapeDtypeStruct((T, D), F32),
                   jax.ShapeDtypeStruct((T, D), BF16),
                   jax.ShapeDtypeStruct((D, T), BF16)],
        compiler_params=_params(("arbitrary",)),
        name="out_projection",
    )(attn, pool, x2d, mod3, norm_g, w_out)


def _oddeven_merge_sort_pairs(n):
    pairs = []

    def merge(lo, hi, r):
        step = r * 2
        if step < hi - lo:
            merge(lo, hi, step)
            merge(lo + r, hi, step)
            for k in range(lo + r, hi - r, step):
                pairs.append((k, k + r))
        else:
            pairs.append((lo, lo + r))

    def sort(lo, hi):
        if hi - lo >= 1:
            mid = lo + (hi - lo) // 2
            sort(lo, mid)
            sort(mid + 1, hi)
            merge(lo, hi, 1)

    sort(0, n - 1)
    return pairs


_SORT16 = _oddeven_merge_sort_pairs(TOPK)


def _sort_desc(vals):
    vals = list(vals)
    for a, b in _SORT16:
        hi, lo = jnp.maximum(vals[a], vals[b]), jnp.minimum(vals[a], vals[b])
        vals[a], vals[b] = hi, lo
    return vals


def _merge_top(a_list, b_list):
    n = TOPK
    c = [jnp.maximum(a_list[k], b_list[n - 1 - k]) for k in range(n)]
    stride = n // 2
    while stride >= 1:
        for k in range(n):
            if (k & stride) == 0:
                hi, lo = jnp.maximum(c[k], c[k + stride]), jnp.minimum(c[k], c[k + stride])
                c[k], c[k + stride] = hi, lo
        stride //= 2
    return c


_CAND_ROWS = [[(i, j) for j in range(TOPK) if (i + 1) * (j + 1) <= TOPK] for i in range(TOPK)]


def _route_kernel(h_ref, wq_ref, keys_ref, e1_ref, e2_ref, thr_ref, qp_scr, top_scr,
                  *, tm, n_heads, n_keys):
    qp = jnp.dot(h_ref[...], wq_ref[...], preferred_element_type=F32)
    qp_scr[...] = qp.astype(BF16)
    n_chunks = tm // LANES
    groups = n_keys // SUBLANES
    assert groups == TOPK

    for h in range(n_heads):
        for half, out_ref in ((0, e1_ref), (1, e2_ref)):
            col0 = (2 * h + half) * n_keys
            s_t = lax.dot_general(keys_ref[h, half], qp_scr[:, col0:col0 + n_keys],
                                  (((1,), (1,)), ((), ())), preferred_element_type=F32)
            out_ref[h] = s_t
            for cidx in range(n_chunks):
                lanes = slice(cidx * LANES, (cidx + 1) * LANES)
                blk = s_t[:, lanes]
                vals = _sort_desc([blk[g * SUBLANES:(g + 1) * SUBLANES, :] for g in range(groups)])
                for shift in (4, 2, 1):
                    partner = [pltpu.roll(v, shift, axis=0) for v in vals]
                    vals = _merge_top(vals, partner)
                for k in range(TOPK):
                    top_scr[half, k, h:h + 1, lanes] = vals[k][0:1, :]

    for cidx in range(n_chunks):
        lanes = slice(cidx * LANES, (cidx + 1) * LANES)
        a = [top_scr[0, k, :, lanes] for k in range(TOPK)]
        b = [top_scr[1, k, :, lanes] for k in range(TOPK)]
        ea = [jnp.exp(v - a[0]) for v in a]
        eb = [jnp.exp(v - b[0]) for v in b]
        pad = jnp.full_like(a[0], -1.0)
        best = None
        for row in _CAND_ROWS:
            lst = [ea[i] * eb[j] for (i, j) in row]
            lst = lst + [pad] * (TOPK - len(lst))
            best = lst if best is None else _merge_top(best, lst)
        z = best[0]
        for k in range(1, TOPK):
            z = z + best[k]
        rz = 1.0 / z
        cut = best[TOPK - 1]
        ebn = [v * rz for v in eb]
        thr = None
        for row in _CAND_ROWS:
            for (i, j) in row:
                sel = ea[i] * eb[j] >= cut
                cand = jnp.where(sel, ea[i] * ebn[j], jnp.inf)
                thr = cand if thr is None else jnp.minimum(thr, cand)
        thr_ref[:, lanes] = thr
        top_scr[0, 0, :, lanes] = a[0]
        top_scr[1, 0, :, lanes] = b[0]
        top_scr[1, 1, :, lanes] = rz

    for h in range(n_heads):
        m1 = top_scr[0, 0, h:h + 1, :]
        m2 = top_scr[1, 0, h:h + 1, :]
        rz = top_scr[1, 1, h:h + 1, :]
        e1_ref[h] = jnp.exp(e1_ref[h] - m1)
        e2_ref[h] = jnp.exp(e2_ref[h] - m2) * rz


def _routing(h2, w_query, sub_keys, *, tm):
    T, D = h2.shape
    n_heads, _, n_keys, kd = sub_keys.shape
    qw = w_query.shape[1]
    kern = functools.partial(_route_kernel, tm=tm, n_heads=n_heads, n_keys=n_keys)
    key_blk = pl.BlockSpec((n_heads, n_keys, tm), lambda i: (0, 0, i))
    return pl.pallas_call(
        kern,
        grid=(T // tm,),
        in_specs=[pl.BlockSpec((tm, D), lambda i: (i, 0)),
                  pl.BlockSpec((D, qw), lambda i: (0, 0)),
                  pl.BlockSpec(sub_keys.shape, lambda i: (0, 0, 0, 0))],
        out_specs=[key_blk, key_blk, pl.BlockSpec((n_heads, tm), lambda i: (0, i))],
        out_shape=[jax.ShapeDtypeStruct((n_heads, n_keys, T), F32),
                   jax.ShapeDtypeStruct((n_heads, n_keys, T), F32),
                   jax.ShapeDtypeStruct((n_heads, T), F32)],
        scratch_shapes=[pltpu.VMEM((tm, qw), BF16),
                        pltpu.VMEM((2, TOPK, n_heads, tm), F32)],
        compiler_params=_params(("arbitrary",)),
        name="peer_routing",
    )(h2, w_query, sub_keys)


def _gelu(a):
    return a * (lax.erf(a * (1.0 / math.sqrt(2.0))) + 1.0) * 0.5


def _expert_kernel(ht_ref, e1_ref, e2_ref, thr_ref, u_ref, vt_ref, x1_ref, mod_ref, g_ref,
                   o_ref, acc_scr, act_scr, pre_scr, *, tm, te, n_heads, n_keys):
    j = pl.program_id(1)
    n_i1 = te // n_keys
    n_chunks = tm // LANES

    @pl.when(j == 0)
    def _():
        acc_scr[...] = jnp.zeros_like(acc_scr)

    n_groups = n_keys // SUBLANES
    i1_per_sub = EXPERT_SUB // n_keys

    for s in range(te // EXPERT_SUB):
        rows = slice(s * EXPERT_SUB, (s + 1) * EXPERT_SUB)
        pre_scr[rows, :] = jnp.dot(u_ref[rows, :], ht_ref[...], preferred_element_type=F32)

    def activation(k, cidx):
        lanes = slice(cidx * LANES, (cidx + 1) * LANES)
        g = _gelu(pre_scr[k * n_keys:(k + 1) * n_keys, lanes])
        tie = g[0:SUBLANES, :] * 0.0
        gate = [None] * n_groups
        for h in range(n_heads):
            e1_rep = e1_ref[h, k:k + 1, lanes] + tie
            thr_row = thr_ref[h:h + 1, lanes]
            for r in range(n_groups):
                p = e1_rep * e2_ref[h, r * SUBLANES:(r + 1) * SUBLANES, lanes]
                sel = jnp.where(p >= thr_row, p, 0.0)
                gate[r] = sel if gate[r] is None else gate[r] + sel
        gate = jnp.concatenate(gate, axis=0)
        act_scr[k * n_keys:(k + 1) * n_keys, lanes] = (g * gate).astype(BF16)

    for c in range(2):
        for k in range(n_i1):
            for cidx in range(c * n_chunks // 2, (c + 1) * n_chunks // 2):
                activation(k, cidx)
        cols = slice(c * tm // 2, (c + 1) * tm // 2)
        acc_scr[:, cols] += jnp.dot(vt_ref[...], act_scr[:, cols], preferred_element_type=F32)

    @pl.when(j == pl.num_programs(1) - 1)
    def _():
        y = acc_scr[...].T
        x2 = x1_ref[...] + mod_ref[0, 5:6, :] * y
        ms = jnp.mean(x2 * x2, axis=-1, keepdims=True)
        o_ref[...] = x2 * lax.rsqrt(ms + NORM_EPS) * g_ref[...]


def _experts(h2t, e1, e2, thr, peer_u, peer_vt, x1, mod3, final_g, *, seq, tm, te):
    D, T = h2t.shape
    n_heads, n_keys, _ = e1.shape
    E = peer_u.shape[0]
    tiles_per_seq = seq // tm
    n_i1 = te // n_keys
    assert n_i1 == SUBLANES, "expert rows of one block fill the sublanes of an f32 tile"
    kern = functools.partial(_expert_kernel, tm=tm, te=te, n_heads=n_heads, n_keys=n_keys)
    row_blk = pl.BlockSpec((n_heads, n_i1, tm), lambda i, j: (0, j, i))
    key_blk = pl.BlockSpec((n_heads, n_keys, tm), lambda i, j: (0, 0, i))
    return pl.pallas_call(
        kern,
        grid=(T // tm, E // te),
        in_specs=[
            pl.BlockSpec((D, tm), lambda i, j: (0, i)),
            row_blk, key_blk,
            pl.BlockSpec((n_heads, tm), lambda i, j: (0, i)),
            pl.BlockSpec((te, D), lambda i, j: (j, 0)),
            pl.BlockSpec((None, D, te), lambda i, j: (j, 0, 0)),
            pl.BlockSpec((tm, D), lambda i, j: (i, 0)),
            pl.BlockSpec((1, 6, D), lambda i, j: (i // tiles_per_seq, 0, 0)),
            pl.BlockSpec((1, D), lambda i, j: (0, 0)),
        ],
        out_specs=pl.BlockSpec((tm, D), lambda i, j: (i, 0)),
        out_shape=jax.ShapeDtypeStruct((T, D), F32),
        scratch_shapes=[pltpu.VMEM((D, tm), F32),
                        pltpu.VMEM((te, tm), BF16),
                        pltpu.VMEM((te, tm), F32)],
        compiler_params=_params(("arbitrary", "arbitrary")),
        name="peer_experts",
    )(h2t, e1, e2, thr, peer_u, peer_vt, x1, mod3, final_g)


def kernel(x, c, positions, w_mod, b_mod, norm1_g, w_in, w_pool, pool_scale, w_out, norm2_g,
           w_query, sub_keys, peer_u, peer_v, final_g):
    B, S, D = x.shape
    assert w_mod.shape[0] == 1, "single-layer stack"
    assert S % ATTN_SPAN == 0
    tm = min(512, S)
    aw = w_in.shape[2] // 4
    x2d = x.reshape(B * S, D)

    mod3 = _modulation(c, w_mod[0], b_mod[0]).reshape(B, 6, D)
    qkv, pool = _in_projection(x2d, mod3, norm1_g[0].reshape(1, D), _rope_tables(positions),
                               w_in[0].astype(BF16), w_pool[0].astype(BF16),
                               pool_scale[0].reshape(1, -1), seq=S, tm=tm)
    attn = _attention(qkv, batch=B, seq=S, attn_width=aw)
    x1, h2, h2t = _out_projection(attn, pool, x2d, mod3, norm2_g[0].reshape(1, D),
                                  w_out[0].astype(BF16), seq=S, tm=tm)
    e1, e2, thr = _routing(h2, w_query[0].astype(BF16), sub_keys[0].astype(BF16), tm=tm)
    te = 1024
    n_exp = peer_v.shape[1]
    vt_blocks = peer_v[0].reshape(n_exp // te, te, D).transpose(0, 2, 1).astype(BF16)
    out = _experts(h2t, e1, e2, thr, peer_u[0].astype(BF16), vt_blocks,
                   x1, mod3, final_g.reshape(1, D), seq=S, tm=tm, te=te)
    return out.reshape(B, S, D)
```

```python
import functools
import math

import jax
import jax.numpy as jnp
from jax import lax
from jax.experimental import pallas as pl
from jax.experimental.pallas import tpu as pltpu

F32 = jnp.float32
BF16 = jnp.bfloat16

HEAD_DIM = 128
ROPE_DIM = HEAD_DIM // 4
ROPE_HALF = ROPE_DIM // 2
ROPE_THETA = 500000.0
POOL_SIZES = (2, 4, 8, 16)
DILATIONS = (1, 4, 16)
ATTN_BLOCK = 128
TOPK = 16
NORM_EPS = 1e-6
NEG_BIG = -1e30

LANES = 128
SUBLANES = 8
VMEM_LIMIT = 56 * 1024 * 1024

ATTN_SPAN = ATTN_BLOCK * max(DILATIONS)
POOL_HALO = 16


def _params(sem, vmem=VMEM_LIMIT):
    return pltpu.CompilerParams(dimension_semantics=sem, vmem_limit_bytes=vmem)


def _mod_kernel(c_ref, w_ref, b_ref, o_ref):
    c = c_ref[...]
    a = c * jax.nn.sigmoid(c)
    o_ref[...] = jnp.dot(a, w_ref[...], preferred_element_type=F32,
                         precision=lax.Precision.HIGHEST) + b_ref[...]


def _modulation(c, w_mod, b_mod):
    B, D = c.shape
    N = w_mod.shape[1]
    tn = 1024
    return pl.pallas_call(
        _mod_kernel,
        grid=(N // tn,),
        in_specs=[pl.BlockSpec((B, D), lambda j: (0, 0)),
                  pl.BlockSpec((D, tn), lambda j: (0, j)),
                  pl.BlockSpec((1, tn), lambda j: (0, j))],
        out_specs=pl.BlockSpec((B, tn), lambda j: (0, j)),
        out_shape=jax.ShapeDtypeStruct((B, N), F32),
        compiler_params=_params(("arbitrary",)),
        name="modulation",
    )(c, w_mod, b_mod.reshape(1, N))


def _rmsnorm_mod(x, g, shift, scale):
    ms = jnp.mean(x * x, axis=-1, keepdims=True)
    y = x * lax.rsqrt(ms + NORM_EPS) * g
    return y * (1.0 + scale) + shift


def _rope_kernel(pos_ref, inv_ref, cos_ref, sin_ref):
    ang = pos_ref[...].astype(F32) * inv_ref[...]
    cos_ref[...] = jnp.cos(ang)
    sin_ref[...] = jnp.sin(ang)


def _rope_tables(positions):
    T = positions.size
    per_row = LANES // ROPE_HALF
    pos_rep = jnp.repeat(positions.reshape(T // per_row, per_row), ROPE_HALF, axis=1)
    inv = ROPE_THETA ** (-jnp.arange(ROPE_HALF, dtype=F32) * 2.0 / ROPE_DIM)
    inv_rep = jnp.tile(inv, per_row).reshape(1, LANES)
    dense = pl.BlockSpec((T // per_row, LANES), lambda: (0, 0))
    cos_c, sin_c = pl.pallas_call(
        _rope_kernel,
        in_specs=[dense, pl.BlockSpec((1, LANES), lambda: (0, 0))],
        out_specs=[dense, dense],
        out_shape=[jax.ShapeDtypeStruct((T // per_row, LANES), F32)] * 2,
        name="rope_tables",
    )(pos_rep, inv_rep)
    cos_h, sin_h = cos_c.reshape(T, ROPE_HALF), sin_c.reshape(T, ROPE_HALF)
    zero_h = jnp.zeros_like(sin_h)
    rest = LANES - ROPE_DIM
    cos = jnp.concatenate([cos_h, cos_h, jnp.ones((T, rest), F32)], axis=1)
    sin_lo = jnp.concatenate([sin_h, zero_h, jnp.zeros((T, rest), F32)], axis=1)
    sin_hi = jnp.concatenate([zero_h, sin_h, jnp.zeros((T, rest), F32)], axis=1)
    return cos, sin_lo, sin_hi


def _inproj_kernel(x_ref, mod_ref, g_ref, cos_ref, sinlo_ref, sinhi_ref, w_ref, wpool_ref, pscale_ref,
                   qkv_ref, pool_ref, h_scr, carry_scr, ext_scr,
                   *, tm, tiles_per_seq, attn_width):
    i = pl.program_id(0)
    j = pl.program_id(1)
    n_heads = attn_width // HEAD_DIM

    @pl.when(j == 0)
    def _():
        h = _rmsnorm_mod(x_ref[...], g_ref[...], mod_ref[0, 0:1, :], mod_ref[0, 1:2, :])
        h_scr[...] = h.astype(BF16)

    z = jnp.dot(h_scr[...], w_ref[...], preferred_element_type=F32)

    def rotary(scale):
        cos = cos_ref[...]
        sin_lo = sinlo_ref[...]
        sin_hi = sinhi_ref[...]
        for hh in range(n_heads):
            zh = z[:, hh * HEAD_DIM:(hh + 1) * HEAD_DIM]
            up = pltpu.roll(zh, HEAD_DIM - ROPE_HALF, axis=1)
            dn = pltpu.roll(zh, ROPE_HALF, axis=1)
            out = zh * cos - up * sin_lo + dn * sin_hi
            if scale is not None:
                out = out * scale
            qkv_ref[:, hh * HEAD_DIM:(hh + 1) * HEAD_DIM] = out

    @pl.when(j == 0)
    def _():
        rotary(HEAD_DIM ** -0.5)

    @pl.when(j == 1)
    def _():
        rotary(None)

    @pl.when(j == 2)
    def _():
        qkv_ref[...] = z

    @pl.when(j == 3)
    def _():
        first = (i % tiles_per_seq) == 0

        @pl.when(first)
        def _():
            carry_scr[...] = jnp.zeros_like(carry_scr)

        ext_scr[0:POOL_HALO, :] = carry_scr[...]
        ext_scr[POOL_HALO:POOL_HALO + tm, :] = z
        carry_scr[...] = z[tm - POOL_HALO:tm, :]
        gw = z.shape[1] // len(POOL_SIZES)
        t_in_seq = (i % tiles_per_seq) * tm + lax.broadcasted_iota(jnp.int32, (tm, gw), 0)
        for gi, p in enumerate(POOL_SIZES):
            cols = slice(gi * gw, (gi + 1) * gw)
            u_g = z[:, cols]
            acc = u_g
            for back in range(1, p):
                acc = acc + ext_scr[POOL_HALO - back:POOL_HALO - back + tm, cols]
            cnt = jnp.minimum(t_in_seq + 1, p).astype(F32)
            r = acc / cnt - u_g
            y = jnp.dot(r.astype(BF16), wpool_ref[gi], preferred_element_type=F32)
            pool_ref[:, cols] = (y * pscale_ref[:, cols]).astype(pool_ref.dtype)


def _in_projection(x2d, mod3, norm_g, rope, w_in, w_pool, pool_scale, *, seq, tm):
    T, D = x2d.shape
    n_in = w_in.shape[1]
    aw = n_in // 4
    tiles_per_seq = seq // tm
    kern = functools.partial(_inproj_kernel, tm=tm, tiles_per_seq=tiles_per_seq, attn_width=aw)
    return pl.pallas_call(
        kern,
        grid=(T // tm, 4),
        in_specs=[
            pl.BlockSpec((tm, D), lambda i, j: (i, 0)),
            pl.BlockSpec((1, 6, D), lambda i, j: (i // tiles_per_seq, 0, 0)),
            pl.BlockSpec((1, D), lambda i, j: (0, 0)),
            pl.BlockSpec((tm, LANES), lambda i, j: (i, 0)),
            pl.BlockSpec((tm, LANES), lambda i, j: (i, 0)),
            pl.BlockSpec((tm, LANES), lambda i, j: (i, 0)),
            pl.BlockSpec((D, aw), lambda i, j: (0, j)),
            pl.BlockSpec(w_pool.shape, lambda i, j: (0, 0, 0)),
            pl.BlockSpec((1, aw), lambda i, j: (0, 0)),
        ],
        out_specs=[
            pl.BlockSpec((tm, aw), lambda i, j: (i, jnp.minimum(j, 2))),
            pl.BlockSpec((tm, aw), lambda i, j: (i, 0)),
        ],
        out_shape=[jax.ShapeDtypeStruct((T, 3 * aw), F32),
                   jax.ShapeDtypeStruct((T, aw), BF16)],
        scratch_shapes=[
            pltpu.VMEM((tm, D), BF16),
            pltpu.VMEM((POOL_HALO, aw), F32),
            pltpu.VMEM((POOL_HALO + tm, aw), F32),
        ],
        compiler_params=_params(("arbitrary", "arbitrary")),
        name="in_projection",
    )(x2d, mod3, norm_g, *rope, w_in, w_pool, pool_scale)


def _strided(start, size, stride):
    return pl.ds(start, size) if stride == 1 else pl.ds(start, size, stride=stride)


def _attn_kernel(q_ref, kp_ref, kc_ref, vp_ref, vc_ref, o_ref, o_scr, l_scr, plane_scr, res_scr):
    n = pl.program_id(2)
    blk = ATTN_BLOCK
    qi = lax.broadcasted_iota(jnp.int32, (blk, 2 * blk), 0)
    kj = lax.broadcasted_iota(jnp.int32, (blk, 2 * blk), 1)
    dist = qi + blk - kj
    band = (dist >= 0) & (dist <= blk)
    band_first = band & ((kj >= blk) | (n > 0))

    def block(q, k_lo, k_hi, v_lo, v_hi, mask):
        k = jnp.concatenate([k_lo, k_hi], axis=0).astype(BF16)
        v = jnp.concatenate([v_lo, v_hi], axis=0).astype(BF16)
        s = lax.dot_general(q.astype(BF16), k, (((1,), (1,)), ((), ())),
                            preferred_element_type=F32)
        s = jnp.where(mask, s, NEG_BIG)
        m = jnp.max(s, axis=-1, keepdims=True)
        p = jnp.exp(s - m)
        den = jnp.sum(p, axis=-1, keepdims=True)
        o = jnp.dot(p.astype(BF16), v, preferred_element_type=F32) / den
        return o, jnp.broadcast_to(m + jnp.log(den), (blk, HEAD_DIM))

    for pi, d in enumerate(DILATIONS[:2]):
        for r in range(d):
            for c in range(ATTN_SPAN // (blk * d)):
                start = r + d * blk * c
                rows = _strided(start, blk, d)
                if c == 0:
                    lo_rows = _strided(ATTN_SPAN + r - d * blk, blk, d)
                    k_lo, v_lo = kp_ref[lo_rows, :], vp_ref[lo_rows, :]
                else:
                    lo_rows = _strided(start - d * blk, blk, d)
                    k_lo, v_lo = kc_ref[lo_rows, :], vc_ref[lo_rows, :]
                o, l = block(q_ref[rows, :], k_lo, kc_ref[rows, :], v_lo, vc_ref[rows, :],
                             band_first if c == 0 else band)
                o_scr[pi, rows, :] = o
                l_scr[pi, rows, :] = l

    sub, per_plane = 4, ATTN_SPAN // 4
    assert DILATIONS[2] == sub * sub and ATTN_SPAN == blk * DILATIONS[2]
    for rho in range(sub):
        plane = pl.ds(rho, per_plane, stride=sub)
        for t, src in enumerate((q_ref, kp_ref, kc_ref, vp_ref, vc_ref)):
            plane_scr[t] = src[plane, :]
        for c2 in range(sub):
            rows = pl.ds(c2, blk, stride=sub)
            o, l = block(plane_scr[0, rows, :], plane_scr[1, rows, :], plane_scr[2, rows, :],
                         plane_scr[3, rows, :], plane_scr[4, rows, :], band_first)
            res_scr[0, rows, :] = o
            res_scr[1, rows, :] = l
        o_scr[2, plane, :] = res_scr[0]
        l_scr[2, plane, :] = res_scr[1]

    l0, l1, l2 = l_scr[0], l_scr[1], l_scr[2]
    lmax = jnp.maximum(jnp.maximum(l0, l1), l2)
    e0, e1, e2 = jnp.exp(l0 - lmax), jnp.exp(l1 - lmax), jnp.exp(l2 - lmax)
    mixed = (e0 * o_scr[0] + e1 * o_scr[1] + e2 * o_scr[2]) / (e0 + e1 + e2)
    o_ref[...] = mixed.astype(o_ref.dtype)


def _attention(qkv, *, batch, seq, attn_width):
    T = qkv.shape[0]
    n_heads = attn_width // HEAD_DIM
    spans = seq // ATTN_SPAN
    blk = (ATTN_SPAN, HEAD_DIM)

    def cur(col0):
        return lambda b, h, n: (b * spans + n, col0 + h)

    def prev(col0):
        return lambda b, h, n: (b * spans + jnp.maximum(n - 1, 0), col0 + h)

    return pl.pallas_call(
        _attn_kernel,
        grid=(batch, n_heads, spans),
        in_specs=[pl.BlockSpec(blk, cur(0)),
                  pl.BlockSpec(blk, prev(n_heads)),
                  pl.BlockSpec(blk, cur(n_heads)),
                  pl.BlockSpec(blk, prev(2 * n_heads)),
                  pl.BlockSpec(blk, cur(2 * n_heads))],
        out_specs=pl.BlockSpec(blk, cur(0)),
        out_shape=jax.ShapeDtypeStruct((T, attn_width), BF16),
        scratch_shapes=[pltpu.VMEM((len(DILATIONS),) + blk, F32),
                        pltpu.VMEM((len(DILATIONS),) + blk, F32),
                        pltpu.VMEM((5, ATTN_SPAN // 4, HEAD_DIM), F32),
                        pltpu.VMEM((2, ATTN_SPAN // 4, HEAD_DIM), F32)],
        compiler_params=_params(("arbitrary", "arbitrary", "arbitrary")),
        name="dilated_attention",
    )(qkv, qkv, qkv, qkv, qkv)


def _outproj_kernel(attn_ref, pool_ref, x_ref, mod_ref, g_ref, w_ref, x1_ref, h2_ref, h2t_ref,
                    *, attn_width):
    mix = jnp.dot(attn_ref[...], w_ref[0:attn_width, :], preferred_element_type=F32)
    mix = mix + jnp.dot(pool_ref[...], w_ref[attn_width:, :], preferred_element_type=F32)
    x1 = x_ref[...] + mod_ref[0, 2:3, :] * mix
    x1_ref[...] = x1
    h2 = _rmsnorm_mod(x1, g_ref[...], mod_ref[0, 3:4, :], mod_ref[0, 4:5, :])
    h2_ref[...] = h2.astype(BF16)
    h2t_ref[...] = h2.T.astype(BF16)


def _out_projection(attn, pool, x2d, mod3, norm_g, w_out, *, seq, tm):
    T, D = x2d.shape
    aw = attn.shape[1]
    pw = pool.shape[1]
    tiles_per_seq = seq // tm
    return pl.pallas_call(
        functools.partial(_outproj_kernel, attn_width=aw),
        grid=(T // tm,),
        in_specs=[
            pl.BlockSpec((tm, aw), lambda i: (i, 0)),
            pl.BlockSpec((tm, pw), lambda i: (i, 0)),
            pl.BlockSpec((tm, D), lambda i: (i, 0)),
            pl.BlockSpec((1, 6, D), lambda i: (i // tiles_per_seq, 0, 0)),
            pl.BlockSpec((1, D), lambda i: (0, 0)),
            pl.BlockSpec((aw + pw, D), lambda i: (0, 0)),
        ],
        out_specs=[pl.BlockSpec((tm, D), lambda i: (i, 0)),
                   pl.BlockSpec((tm, D), lambda i: (i, 0)),
                   pl.BlockSpec((D, tm), lambda i: (0, i))],
        out_shape=[jax.ShapeDtypeStruct((T, D), F32),
                   jax.ShapeDtypeStruct((T, D), BF16),
                   jax.ShapeDtypeStruct((D, T), BF16)],
        compiler_params=_params(("arbitrary",)),
        name="out_projection",
    )(attn, pool, x2d, mod3, norm_g, w_out)


def _oddeven_merge_sort_pairs(n):
    pairs = []

    def merge(lo, hi, r):
        step = r * 2
        if step < hi - lo:
            merge(lo, hi, step)
            merge(lo + r, hi, step)
            for k in range(lo + r, hi - r, step):
                pairs.append((k, k + r))
        else:
            pairs.append((lo, lo + r))

    def sort(lo, hi):
        if hi - lo >= 1:
            mid = lo + (hi - lo) // 2
            sort(lo, mid)
            sort(mid + 1, hi)
            merge(lo, hi, 1)

    sort(0, n - 1)
    return pairs


_SORT16 = _oddeven_merge_sort_pairs(TOPK)


def _sort_desc(vals):
    vals = list(vals)
    for a, b in _SORT16:
        hi, lo = jnp.maximum(vals[a], vals[b]), jnp.minimum(vals[a], vals[b])
        vals[a], vals[b] = hi, lo
    return vals


def _merge_top(a_list, b_list):
    n = TOPK
    c = [jnp.maximum(a_list[k], b_list[n - 1 - k]) for k in range(n)]
    stride = n // 2
    while stride >= 1:
        for k in range(n):
            if (k & stride) == 0:
                hi, lo = jnp.maximum(c[k], c[k + stride]), jnp.minimum(c[k], c[k + stride])
                c[k], c[k + stride] = hi, lo
        stride //= 2
    return c


_CAND_ROWS = [[(i, j) for j in range(TOPK) if (i + 1) * (j + 1) <= TOPK] for i in range(TOPK)]


def _route_kernel(h_ref, wq_ref, keys_ref, e1_ref, e2_ref, thr_ref, qp_scr, top_scr,
                  *, tm, n_heads, n_keys):
    qp = jnp.dot(h_ref[...], wq_ref[...], preferred_element_type=F32)
    qp_scr[...] = qp.astype(BF16)
    n_chunks = tm // LANES
    groups = n_keys // SUBLANES
    assert groups == TOPK

    for h in range(n_heads):
        for half, out_ref in ((0, e1_ref), (1, e2_ref)):
            col0 = (2 * h + half) * n_keys
            s_t = lax.dot_general(keys_ref[h, half], qp_scr[:, col0:col0 + n_keys],
                                  (((1,), (1,)), ((), ())), preferred_element_type=F32)
            out_ref[h] = s_t
            for cidx in range(n_chunks):
                lanes = slice(cidx * LANES, (cidx + 1) * LANES)
                blk = s_t[:, lanes]
                vals = _sort_desc([blk[g * SUBLANES:(g + 1) * SUBLANES, :] for g in range(groups)])
                for shift in (4, 2, 1):
                    partner = [pltpu.roll(v, shift, axis=0) for v in vals]
                    vals = _merge_top(vals, partner)
                for k in range(TOPK):
                    top_scr[half, k, h:h + 1, lanes] = vals[k][0:1, :]

    for cidx in range(n_chunks):
        lanes = slice(cidx * LANES, (cidx + 1) * LANES)
        a = [top_scr[0, k, :, lanes] for k in range(TOPK)]
        b = [top_scr[1, k, :, lanes] for k in range(TOPK)]
        ea = [jnp.exp(v - a[0]) for v in a]
        eb = [jnp.exp(v - b[0]) for v in b]
        pad = jnp.full_like(a[0], -1.0)
        best = None
        for row in _CAND_ROWS:
            lst = [ea[i] * eb[j] for (i, j) in row]
            lst = lst + [pad] * (TOPK - len(lst))
            best = lst if best is None else _merge_top(best, lst)
        z = best[0]
        for k in range(1, TOPK):
            z = z + best[k]
        rz = 1.0 / z
        cut = best[TOPK - 1]
        ebn = [v * rz for v in eb]
        thr = None
        for row in _CAND_ROWS:
            for (i, j) in row:
                sel = ea[i] * eb[j] >= cut
                cand = jnp.where(sel, ea[i] * ebn[j], jnp.inf)
                thr = cand if thr is None else jnp.minimum(thr, cand)
        thr_ref[:, lanes] = thr
        top_scr[0, 0, :, lanes] = a[0]
        top_scr[1, 0, :, lanes] = b[0]
        top_scr[1, 1, :, lanes] = rz

    for h in range(n_heads):
        m1 = top_scr[0, 0, h:h + 1, :]
        m2 = top_scr[1, 0, h:h + 1, :]
        rz = top_scr[1, 1, h:h + 1, :]
        e1_ref[h] = jnp.exp(e1_ref[h] - m1)
        e2_ref[h] = jnp.exp(e2_ref[h] - m2) * rz


def _routing(h2, w_query, sub_keys, *, tm):
    T, D = h2.shape
    n_heads, _, n_keys, kd = sub_keys.shape
    qw = w_query.shape[1]
    kern = functools.partial(_route_kernel, tm=tm, n_heads=n_heads, n_keys=n_keys)
    key_blk = pl.BlockSpec((n_heads, n_keys, tm), lambda i: (0, 0, i))
    return pl.pallas_call(
        kern,
        grid=(T // tm,),
        in_specs=[pl.BlockSpec((tm, D), lambda i: (i, 0)),
                  pl.BlockSpec((D, qw), lambda i: (0, 0)),
                  pl.BlockSpec(sub_keys.shape, lambda i: (0, 0, 0, 0))],
        out_specs=[key_blk, key_blk, pl.BlockSpec((n_heads, tm), lambda i: (0, i))],
        out_shape=[jax.ShapeDtypeStruct((n_heads, n_keys, T), F32),
                   jax.ShapeDtypeStruct((n_heads, n_keys, T), F32),
                   jax.ShapeDtypeStruct((n_heads, T), F32)],
        scratch_shapes=[pltpu.VMEM((tm, qw), BF16),
                        pltpu.VMEM((2, TOPK, n_heads, tm), F32)],
        compiler_params=_params(("arbitrary",)),
        name="peer_routing",
    )(h2, w_query, sub_keys)


def _gelu(a):
    return a * (lax.erf(a * (1.0 / math.sqrt(2.0))) + 1.0) * 0.5


def _expert_kernel(ht_ref, e1_ref, e2_ref, thr_ref, u_ref, vt_ref, x1_ref, mod_ref, g_ref,
                   o_ref, acc_scr, act_scr, *, tm, te, n_heads, n_keys):
    j = pl.program_id(1)
    n_i1 = te // n_keys
    n_chunks = tm // LANES

    @pl.when(j == 0)
    def _():
        acc_scr[...] = jnp.zeros_like(acc_scr)

    a_t = jnp.dot(u_ref[...], ht_ref[...], preferred_element_type=F32)
    for k in range(n_i1):
        for cidx in range(n_chunks):
            lanes = slice(cidx * LANES, (cidx + 1) * LANES)
            gate = None
            for h in range(n_heads):
                p = e1_ref[h, k:k + 1, lanes] * e2_ref[h, :, lanes]
                sel = jnp.where(p >= thr_ref[h:h + 1, lanes], p, 0.0)
                gate = sel if gate is None else gate + sel
            a_blk = a_t[k * n_keys:(k + 1) * n_keys, lanes]
            act_scr[k * n_keys:(k + 1) * n_keys, lanes] = (_gelu(a_blk) * gate).astype(BF16)

    for c in range(2):
        cols = slice(c * tm // 2, (c + 1) * tm // 2)
        acc_scr[:, cols] += jnp.dot(vt_ref[...], act_scr[:, cols], preferred_element_type=F32)

    @pl.when(j == pl.num_programs(1) - 1)
    def _():
        y = acc_scr[...].T
        x2 = x1_ref[...] + mod_ref[0, 5:6, :] * y
        ms = jnp.mean(x2 * x2, axis=-1, keepdims=True)
        o_ref[...] = x2 * lax.rsqrt(ms + NORM_EPS) * g_ref[...]


def _experts(h2t, e1, e2, thr, peer_u, peer_vt, x1, mod3, final_g, *, seq, tm, te):
    D, T = h2t.shape
    n_heads, n_keys, _ = e1.shape
    E = peer_u.shape[0]
    tiles_per_seq = seq // tm
    n_i1 = te // n_keys
    assert n_i1 == SUBLANES, "expert rows of one block fill the sublanes of an f32 tile"
    kern = functools.partial(_expert_kernel, tm=tm, te=te, n_heads=n_heads, n_keys=n_keys)
    row_blk = pl.BlockSpec((n_heads, n_i1, tm), lambda i, j: (0, j, i))
    key_blk = pl.BlockSpec((n_heads, n_keys, tm), lambda i, j: (0, 0, i))
    return pl.pallas_call(
        kern,
        grid=(T // tm, E // te),
        in_specs=[
            pl.BlockSpec((D, tm), lambda i, j: (0, i)),
            row_blk, key_blk,
            pl.BlockSpec((n_heads, tm), lambda i, j: (0, i)),
            pl.BlockSpec((te, D), lambda i, j: (j, 0)),
            pl.BlockSpec((None, D, te), lambda i, j: (j, 0, 0)),
            pl.BlockSpec((tm, D), lambda i, j: (i, 0)),
            pl.BlockSpec((1, 6, D), lambda i, j: (i // tiles_per_seq, 0, 0)),
            pl.BlockSpec((1, D), lambda i, j: (0, 0)),
        ],
        out_specs=pl.BlockSpec((tm, D), lambda i, j: (i, 0)),
        out_shape=jax.ShapeDtypeStruct((T, D), F32),
        scratch_shapes=[pltpu.VMEM((D, tm), F32),
                        pltpu.VMEM((te, tm), BF16)],
        compiler_params=_params(("arbitrary", "arbitrary")),
        name="peer_experts",
    )(h2t, e1, e2, thr, peer_u, peer_vt, x1, mod3, final_g)


def kernel(x, c, positions, w_mod, b_mod, norm1_g, w_in, w_pool, pool_scale, w_out, norm2_g,
           w_query, sub_keys, peer_u, peer_v, final_g):
    B, S, D = x.shape
    assert w_mod.shape[0] == 1, "single-layer stack"
    assert S % ATTN_SPAN == 0
    tm = min(512, S)
    aw = w_in.shape[2] // 4
    x2d = x.reshape(B * S, D)

    mod3 = _modulation(c, w_mod[0], b_mod[0]).reshape(B, 6, D)
    qkv, pool = _in_projection(x2d, mod3, norm1_g[0].reshape(1, D), _rope_tables(positions),
                               w_in[0].astype(BF16), w_pool[0].astype(BF16),
                               pool_scale[0].reshape(1, -1), seq=S, tm=tm)
    attn = _attention(qkv, batch=B, seq=S, attn_width=aw)
    x1, h2, h2t = _out_projection(attn, pool, x2d, mod3, norm2_g[0].reshape(1, D),
                                  w_out[0].astype(BF16), seq=S, tm=tm)
    e1, e2, thr = _routing(h2, w_query[0].astype(BF16), sub_keys[0].astype(BF16), tm=tm)
    te = 1024
    n_exp = peer_v.shape[1]
    vt_blocks = peer_v[0].reshape(n_exp // te, te, D).transpose(0, 2, 1).astype(BF16)
    out = _experts(h2t, e1, e2, thr, peer_u[0].astype(BF16), vt_blocks,
                   x1, mod3, final_g.reshape(1, D), seq=S, tm=tm, te=te)
    return out.reshape(B, S, D)
```

```python
import functools
import math

import jax
import jax.numpy as jnp
from jax import lax
from jax.experimental import pallas as pl
from jax.experimental.pallas import tpu as pltpu

F32 = jnp.float32
BF16 = jnp.bfloat16

HEAD_DIM = 128
ROPE_DIM = HEAD_DIM // 4
ROPE_HALF = ROPE_DIM // 2
ROPE_THETA = 500000.0
POOL_SIZES = (2, 4, 8, 16)
DILATIONS = (1, 4, 16)
ATTN_BLOCK = 128
TOPK = 16
NORM_EPS = 1e-6
NEG_BIG = -1e30

LANES = 128
SUBLANES = 8
VMEM_LIMIT = 56 * 1024 * 1024

ATTN_SPAN = ATTN_BLOCK * max(DILATIONS)
POOL_HALO = 16


def _params(sem, vmem=VMEM_LIMIT):
    return pltpu.CompilerParams(dimension_semantics=sem, vmem_limit_bytes=vmem)


def _mod_kernel(c_ref, w_ref, b_ref, o_ref):
    c = c_ref[...]
    a = c * jax.nn.sigmoid(c)
    o_ref[...] = jnp.dot(a, w_ref[...], preferred_element_type=F32,
                         precision=lax.Precision.HIGHEST) + b_ref[...]


def _modulation(c, w_mod, b_mod):
    B, D = c.shape
    N = w_mod.shape[1]
    tn = 1024
    return pl.pallas_call(
        _mod_kernel,
        grid=(N // tn,),
        in_specs=[pl.BlockSpec((B, D), lambda j: (0, 0)),
                  pl.BlockSpec((D, tn), lambda j: (0, j)),
                  pl.BlockSpec((1, tn), lambda j: (0, j))],
        out_specs=pl.BlockSpec((B, tn), lambda j: (0, j)),
        out_shape=jax.ShapeDtypeStruct((B, N), F32),
        compiler_params=_params(("arbitrary",)),
        name="modulation",
    )(c, w_mod, b_mod.reshape(1, N))


def _rmsnorm_mod(x, g, shift, scale):
    ms = jnp.mean(x * x, axis=-1, keepdims=True)
    y = x * lax.rsqrt(ms + NORM_EPS) * g
    return y * (1.0 + scale) + shift


def _rope_kernel(pos_ref, inv_ref, cos_ref, sin_ref):
    ang = pos_ref[...].astype(F32) * inv_ref[...]
    cos_ref[...] = jnp.cos(ang)
    sin_ref[...] = jnp.sin(ang)


def _rope_tables(positions):
    T = positions.size
    per_row = LANES // ROPE_HALF
    pos_rep = jnp.repeat(positions.reshape(T // per_row, per_row), ROPE_HALF, axis=1)
    inv = ROPE_THETA ** (-jnp.arange(ROPE_HALF, dtype=F32) * 2.0 / ROPE_DIM)
    inv_rep = jnp.tile(inv, per_row).reshape(1, LANES)
    dense = pl.BlockSpec((T // per_row, LANES), lambda: (0, 0))
    cos_c, sin_c = pl.pallas_call(
        _rope_kernel,
        in_specs=[dense, pl.BlockSpec((1, LANES), lambda: (0, 0))],
        out_specs=[dense, dense],
        out_shape=[jax.ShapeDtypeStruct((T // per_row, LANES), F32)] * 2,
        name="rope_tables",
    )(pos_rep, inv_rep)
    cos_h, sin_h = cos_c.reshape(T, ROPE_HALF), sin_c.reshape(T, ROPE_HALF)
    zero_h = jnp.zeros_like(sin_h)
    rest = LANES - ROPE_DIM
    cos = jnp.concatenate([cos_h, cos_h, jnp.ones((T, rest), F32)], axis=1)
    sin_lo = jnp.concatenate([sin_h, zero_h, jnp.zeros((T, rest), F32)], axis=1)
    sin_hi = jnp.concatenate([zero_h, sin_h, jnp.zeros((T, rest), F32)], axis=1)
    return cos, sin_lo, sin_hi


def _inproj_kernel(x_ref, mod_ref, g_ref, cos_ref, sinlo_ref, sinhi_ref, w_ref, wpool_ref, pscale_ref,
                   qkv_ref, pool_ref, carry_scr, ext_scr,
                   *, tm, tiles_per_seq, attn_width):
    i = pl.program_id(0)
    aw = attn_width
    n_heads = aw // HEAD_DIM
    h = _rmsnorm_mod(x_ref[...], g_ref[...], mod_ref[0, 0:1, :], mod_ref[0, 1:2, :]).astype(BF16)

    def project(j):
        return jnp.dot(h, w_ref[:, j * aw:(j + 1) * aw], preferred_element_type=F32)

    def rotary(z, col0, scale):
        cos = cos_ref[...]
        sin_lo = sinlo_ref[...]
        sin_hi = sinhi_ref[...]
        for hh in range(n_heads):
            zh = z[:, hh * HEAD_DIM:(hh + 1) * HEAD_DIM]
            up = pltpu.roll(zh, HEAD_DIM - ROPE_HALF, axis=1)
            dn = pltpu.roll(zh, ROPE_HALF, axis=1)
            out = zh * cos - up * sin_lo + dn * sin_hi
            if scale is not None:
                out = out * scale
            qkv_ref[:, col0 + hh * HEAD_DIM:col0 + (hh + 1) * HEAD_DIM] = out

    rotary(project(0), 0, HEAD_DIM ** -0.5)
    rotary(project(1), aw, None)
    qkv_ref[:, 2 * aw:3 * aw] = project(2)

    u = project(3)
    first = (i % tiles_per_seq) == 0

    @pl.when(first)
    def _():
        carry_scr[...] = jnp.zeros_like(carry_scr)

    ext_scr[0:POOL_HALO, :] = carry_scr[...]
    ext_scr[POOL_HALO:POOL_HALO + tm, :] = u
    carry_scr[...] = u[tm - POOL_HALO:tm, :]
    gw = aw // len(POOL_SIZES)
    t_in_seq = (i % tiles_per_seq) * tm + lax.broadcasted_iota(jnp.int32, (tm, gw), 0)
    for gi, p in enumerate(POOL_SIZES):
        cols = slice(gi * gw, (gi + 1) * gw)
        u_g = u[:, cols]
        acc = u_g
        for back in range(1, p):
            acc = acc + ext_scr[POOL_HALO - back:POOL_HALO - back + tm, cols]
        cnt = jnp.minimum(t_in_seq + 1, p).astype(F32)
        r = acc / cnt - u_g
        y = jnp.dot(r.astype(BF16), wpool_ref[gi], preferred_element_type=F32)
        pool_ref[:, cols] = (y * pscale_ref[:, cols]).astype(pool_ref.dtype)


def _in_projection(x2d, mod3, norm_g, rope, w_in, w_pool, pool_scale, *, seq, tm):
    T, D = x2d.shape
    n_in = w_in.shape[1]
    aw = n_in // 4
    tiles_per_seq = seq // tm
    kern = functools.partial(_inproj_kernel, tm=tm, tiles_per_seq=tiles_per_seq, attn_width=aw)
    return pl.pallas_call(
        kern,
        grid=(T // tm,),
        in_specs=[
            pl.BlockSpec((tm, D), lambda i: (i, 0)),
            pl.BlockSpec((1, 6, D), lambda i: (i // tiles_per_seq, 0, 0)),
            pl.BlockSpec((1, D), lambda i: (0, 0)),
            pl.BlockSpec((tm, LANES), lambda i: (i, 0)),
            pl.BlockSpec((tm, LANES), lambda i: (i, 0)),
            pl.BlockSpec((tm, LANES), lambda i: (i, 0)),
            pl.BlockSpec((D, n_in), lambda i: (0, 0), pipeline_mode=pl.Buffered(1)),
            pl.BlockSpec(w_pool.shape, lambda i: (0, 0, 0)),
            pl.BlockSpec((1, aw), lambda i: (0, 0)),
        ],
        out_specs=[
            pl.BlockSpec((tm, 3 * aw), lambda i: (i, 0)),
            pl.BlockSpec((tm, aw), lambda i: (i, 0)),
        ],
        out_shape=[jax.ShapeDtypeStruct((T, 3 * aw), F32),
                   jax.ShapeDtypeStruct((T, aw), BF16)],
        scratch_shapes=[
            pltpu.VMEM((POOL_HALO, aw), F32),
            pltpu.VMEM((POOL_HALO + tm, aw), F32),
        ],
        compiler_params=_params(("arbitrary",)),
        name="in_projection",
    )(x2d, mod3, norm_g, *rope, w_in, w_pool, pool_scale)


def _strided(start, size, stride):
    return pl.ds(start, size) if stride == 1 else pl.ds(start, size, stride=stride)


def _attn_kernel(q_ref, kp_ref, kc_ref, vp_ref, vc_ref, o_ref, o_scr, l_scr, plane_scr, res_scr):
    n = pl.program_id(2)
    blk = ATTN_BLOCK
    qi = lax.broadcasted_iota(jnp.int32, (blk, 2 * blk), 0)
    kj = lax.broadcasted_iota(jnp.int32, (blk, 2 * blk), 1)
    dist = qi + blk - kj
    band = (dist >= 0) & (dist <= blk)
    band_first = band & ((kj >= blk) | (n > 0))

    def block(q, k_lo, k_hi, v_lo, v_hi, mask):
        k = jnp.concatenate([k_lo, k_hi], axis=0).astype(BF16)
        v = jnp.concatenate([v_lo, v_hi], axis=0).astype(BF16)
        s = lax.dot_general(q.astype(BF16), k, (((1,), (1,)), ((), ())),
                            preferred_element_type=F32)
        s = jnp.where(mask, s, NEG_BIG)
        m = jnp.max(s, axis=-1, keepdims=True)
        p = jnp.exp(s - m)
        den = jnp.sum(p, axis=-1, keepdims=True)
        o = jnp.dot(p.astype(BF16), v, preferred_element_type=F32) / den
        return o, jnp.broadcast_to(m + jnp.log(den), (blk, HEAD_DIM))

    for pi, d in enumerate(DILATIONS[:2]):
        for r in range(d):
            for c in range(ATTN_SPAN // (blk * d)):
                start = r + d * blk * c
                rows = _strided(start, blk, d)
                if c == 0:
                    lo_rows = _strided(ATTN_SPAN + r - d * blk, blk, d)
                    k_lo, v_lo = kp_ref[lo_rows, :], vp_ref[lo_rows, :]
                else:
                    lo_rows = _strided(start - d * blk, blk, d)
                    k_lo, v_lo = kc_ref[lo_rows, :], vc_ref[lo_rows, :]
                o, l = block(q_ref[rows, :], k_lo, kc_ref[rows, :], v_lo, vc_ref[rows, :],
                             band_first if c == 0 else band)
                o_scr[pi, rows, :] = o
                l_scr[pi, rows, :] = l

    sub, per_plane = 4, ATTN_SPAN // 4
    assert DILATIONS[2] == sub * sub and ATTN_SPAN == blk * DILATIONS[2]
    for rho in range(sub):
        plane = pl.ds(rho, per_plane, stride=sub)
        for t, src in enumerate((q_ref, kp_ref, kc_ref, vp_ref, vc_ref)):
            plane_scr[t] = src[plane, :]
        for c2 in range(sub):
            rows = pl.ds(c2, blk, stride=sub)
            o, l = block(plane_scr[0, rows, :], plane_scr[1, rows, :], plane_scr[2, rows, :],
                         plane_scr[3, rows, :], plane_scr[4, rows, :], band_first)
            res_scr[0, rows, :] = o
            res_scr[1, rows, :] = l
        o_scr[2, plane, :] = res_scr[0]
        l_scr[2, plane, :] = res_scr[1]

    l0, l1, l2 = l_scr[0], l_scr[1], l_scr[2]
    lmax = jnp.maximum(jnp.maximum(l0, l1), l2)
    e0, e1, e2 = jnp.exp(l0 - lmax), jnp.exp(l1 - lmax), jnp.exp(l2 - lmax)
    mixed = (e0 * o_scr[0] + e1 * o_scr[1] + e2 * o_scr[2]) / (e0 + e1 + e2)
    o_ref[...] = mixed.astype(o_ref.dtype)


def _attention(qkv, *, batch, seq, attn_width):
    T = qkv.shape[0]
    n_heads = attn_width // HEAD_DIM
    spans = seq // ATTN_SPAN
    blk = (ATTN_SPAN, HEAD_DIM)

    def cur(col0):
        return lambda b, h, n: (b * spans + n, col0 + h)

    def prev(col0):
        return lambda b, h, n: (b * spans + jnp.maximum(n - 1, 0), col0 + h)

    return pl.pallas_call(
        _attn_kernel,
        grid=(batch, n_heads, spans),
        in_specs=[pl.BlockSpec(blk, cur(0)),
                  pl.BlockSpec(blk, prev(n_heads)),
                  pl.BlockSpec(blk, cur(n_heads)),
                  pl.BlockSpec(blk, prev(2 * n_heads)),
                  pl.BlockSpec(blk, cur(2 * n_heads))],
        out_specs=pl.BlockSpec(blk, cur(0)),
        out_shape=jax.ShapeDtypeStruct((T, attn_width), BF16),
        scratch_shapes=[pltpu.VMEM((len(DILATIONS),) + blk, F32),
                        pltpu.VMEM((len(DILATIONS),) + blk, F32),
                        pltpu.VMEM((5, ATTN_SPAN // 4, HEAD_DIM), F32),
                        pltpu.VMEM((2, ATTN_SPAN // 4, HEAD_DIM), F32)],
        compiler_params=_params(("arbitrary", "arbitrary", "arbitrary")),
        name="dilated_attention",
    )(qkv, qkv, qkv, qkv, qkv)


def _outproj_kernel(attn_ref, pool_ref, x_ref, mod_ref, g_ref, w_ref, x1_ref, h2_ref, *, attn_width):
    mix = jnp.dot(attn_ref[...], w_ref[0:attn_width, :], preferred_element_type=F32)
    mix = mix + jnp.dot(pool_ref[...], w_ref[attn_width:, :], preferred_element_type=F32)
    x1 = x_ref[...] + mod_ref[0, 2:3, :] * mix
    x1_ref[...] = x1
    h2 = _rmsnorm_mod(x1, g_ref[...], mod_ref[0, 3:4, :], mod_ref[0, 4:5, :])
    h2_ref[...] = h2.astype(BF16)


def _out_projection(attn, pool, x2d, mod3, norm_g, w_out, *, seq, tm):
    T, D = x2d.shape
    aw = attn.shape[1]
    pw = pool.shape[1]
    tiles_per_seq = seq // tm
    return pl.pallas_call(
        functools.partial(_outproj_kernel, attn_width=aw),
        grid=(T // tm,),
        in_specs=[
            pl.BlockSpec((tm, aw), lambda i: (i, 0)),
            pl.BlockSpec((tm, pw), lambda i: (i, 0)),
            pl.BlockSpec((tm, D), lambda i: (i, 0)),
            pl.BlockSpec((1, 6, D), lambda i: (i // tiles_per_seq, 0, 0)),
            pl.BlockSpec((1, D), lambda i: (0, 0)),
            pl.BlockSpec((aw + pw, D), lambda i: (0, 0)),
        ],
        out_specs=[pl.BlockSpec((tm, D), lambda i: (i, 0)),
                   pl.BlockSpec((tm, D), lambda i: (i, 0))],
        out_shape=[jax.ShapeDtypeStruct((T, D), F32),
                   jax.ShapeDtypeStruct((T, D), BF16)],
        compiler_params=_params(("arbitrary",)),
        name="out_projection",
    )(attn, pool, x2d, mod3, norm_g, w_out)


def _oddeven_merge_sort_pairs(n):
    pairs = []

    def merge(lo, hi, r):
        step = r * 2
        if step < hi - lo:
            merge(lo, hi, step)
            merge(lo + r, hi, step)
            for k in range(lo + r, hi - r, step):
                pairs.append((k, k + r))
        else:
            pairs.append((lo, lo + r))

    def sort(lo, hi):
        if hi - lo >= 1:
            mid = lo + (hi - lo) // 2
            sort(lo, mid)
            sort(mid + 1, hi)
            merge(lo, hi, 1)

    sort(0, n - 1)
    return pairs


_SORT16 = _oddeven_merge_sort_pairs(TOPK)


def _sort_desc(vals):
    vals = list(vals)
    for a, b in _SORT16:
        hi, lo = jnp.maximum(vals[a], vals[b]), jnp.minimum(vals[a], vals[b])
        vals[a], vals[b] = hi, lo
    return vals


def _merge_top(a_list, b_list):
    n = TOPK
    c = [jnp.maximum(a_list[k], b_list[n - 1 - k]) for k in range(n)]
    stride = n // 2
    while stride >= 1:
        for k in range(n):
            if (k & stride) == 0:
                hi, lo = jnp.maximum(c[k], c[k + stride]), jnp.minimum(c[k], c[k + stride])
                c[k], c[k + stride] = hi, lo
        stride //= 2
    return c


_CAND_ROWS = [[(i, j) for j in range(TOPK) if (i + 1) * (j + 1) <= TOPK] for i in range(TOPK)]


def _route_kernel(h_ref, wq_ref, keys_ref, e1_ref, e2_ref, thr_ref, qp_scr, top_scr,
                  *, tm, n_heads, n_keys):
    qp = jnp.dot(h_ref[...], wq_ref[...], preferred_element_type=F32)
    qp_scr[...] = qp.astype(BF16)
    n_chunks = tm // LANES
    groups = n_keys // SUBLANES
    assert groups == TOPK

    for h in range(n_heads):
        for half, out_ref in ((0, e1_ref), (1, e2_ref)):
            col0 = (2 * h + half) * n_keys
            s_t = lax.dot_general(keys_ref[h, half], qp_scr[:, col0:col0 + n_keys],
                                  (((1,), (1,)), ((), ())), preferred_element_type=F32)
            out_ref[h] = s_t
            for cidx in range(n_chunks):
                lanes = slice(cidx * LANES, (cidx + 1) * LANES)
                blk = s_t[:, lanes]
                vals = _sort_desc([blk[g * SUBLANES:(g + 1) * SUBLANES, :] for g in range(groups)])
                for shift in (4, 2, 1):
                    partner = [pltpu.roll(v, shift, axis=0) for v in vals]
                    vals = _merge_top(vals, partner)
                for k in range(TOPK):
                    top_scr[half, k, h:h + 1, lanes] = vals[k][0:1, :]

    for cidx in range(n_chunks):
        lanes = slice(cidx * LANES, (cidx + 1) * LANES)
        a = [top_scr[0, k, :, lanes] for k in range(TOPK)]
        b = [top_scr[1, k, :, lanes] for k in range(TOPK)]
        ea = [jnp.exp(v - a[0]) for v in a]
        eb = [jnp.exp(v - b[0]) for v in b]
        pad = jnp.full_like(a[0], -1.0)
        best = None
        for row in _CAND_ROWS:
            lst = [ea[i] * eb[j] for (i, j) in row]
            lst = lst + [pad] * (TOPK - len(lst))
            best = lst if best is None else _merge_top(best, lst)
        z = best[0]
        for k in range(1, TOPK):
            z = z + best[k]
        rz = 1.0 / z
        cut = best[TOPK - 1]
        ebn = [v * rz for v in eb]
        thr = None
        for row in _CAND_ROWS:
            for (i, j) in row:
                sel = ea[i] * eb[j] >= cut
                cand = jnp.where(sel, ea[i] * ebn[j], jnp.inf)
                thr = cand if thr is None else jnp.minimum(thr, cand)
        thr_ref[:, lanes] = thr
        top_scr[0, 0, :, lanes] = a[0]
        top_scr[1, 0, :, lanes] = b[0]
        top_scr[1, 1, :, lanes] = rz

    for h in range(n_heads):
        m1 = top_scr[0, 0, h:h + 1, :]
        m2 = top_scr[1, 0, h:h + 1, :]
        rz = top_scr[1, 1, h:h + 1, :]
        e1_ref[h] = jnp.exp(e1_ref[h] - m1)
        e2_ref[h] = jnp.exp(e2_ref[h] - m2) * rz


def _routing(h2, w_query, sub_keys, *, tm):
    T, D = h2.shape
    n_heads, _, n_keys, kd = sub_keys.shape
    qw = w_query.shape[1]
    kern = functools.partial(_route_kernel, tm=tm, n_heads=n_heads, n_keys=n_keys)
    key_blk = pl.BlockSpec((n_heads, n_keys, tm), lambda i: (0, 0, i))
    return pl.pallas_call(
        kern,
        grid=(T // tm,),
        in_specs=[pl.BlockSpec((tm, D), lambda i: (i, 0)),
                  pl.BlockSpec((D, qw), lambda i: (0, 0)),
                  pl.BlockSpec(sub_keys.shape, lambda i: (0, 0, 0, 0))],
        out_specs=[key_blk, key_blk, pl.BlockSpec((n_heads, tm), lambda i: (0, i))],
        out_shape=[jax.ShapeDtypeStruct((n_heads, n_keys, T), F32),
                   jax.ShapeDtypeStruct((n_heads, n_keys, T), F32),
                   jax.ShapeDtypeStruct((n_heads, T), F32)],
        scratch_shapes=[pltpu.VMEM((tm, qw), BF16),
                        pltpu.VMEM((2, TOPK, n_heads, tm), F32)],
        compiler_params=_params(("arbitrary",)),
        name="peer_routing",
    )(h2, w_query, sub_keys)


def _gelu(a):
    return a * (lax.erf(a * (1.0 / math.sqrt(2.0))) + 1.0) * 0.5


def _expert_kernel(h_ref, e1_ref, e2_ref, thr_ref, u_ref, vt_ref, x1_ref, mod_ref, g_ref,
                   o_ref, acc_scr, act_scr, *, tm, te, n_heads, n_keys):
    j = pl.program_id(1)
    n_i1 = te // n_keys
    n_chunks = tm // LANES

    @pl.when(j == 0)
    def _():
        acc_scr[...] = jnp.zeros_like(acc_scr)

    a_t = lax.dot_general(u_ref[...], h_ref[...], (((1,), (1,)), ((), ())),
                          preferred_element_type=F32)
    for k in range(n_i1):
        for cidx in range(n_chunks):
            lanes = slice(cidx * LANES, (cidx + 1) * LANES)
            gate = None
            for h in range(n_heads):
                p = e1_ref[h, k:k + 1, lanes] * e2_ref[h, :, lanes]
                sel = jnp.where(p >= thr_ref[h:h + 1, lanes], p, 0.0)
                gate = sel if gate is None else gate + sel
            a_blk = a_t[k * n_keys:(k + 1) * n_keys, lanes]
            act_scr[k * n_keys:(k + 1) * n_keys, lanes] = (_gelu(a_blk) * gate).astype(BF16)

    for c in range(2):
        cols = slice(c * tm // 2, (c + 1) * tm // 2)
        acc_scr[:, cols] += jnp.dot(vt_ref[...], act_scr[:, cols], preferred_element_type=F32)

    @pl.when(j == pl.num_programs(1) - 1)
    def _():
        y = acc_scr[...].T
        x2 = x1_ref[...] + mod_ref[0, 5:6, :] * y
        ms = jnp.mean(x2 * x2, axis=-1, keepdims=True)
        o_ref[...] = x2 * lax.rsqrt(ms + NORM_EPS) * g_ref[...]


def _experts(h2, e1, e2, thr, peer_u, peer_vt, x1, mod3, final_g, *, seq, tm, te):
    T, D = h2.shape
    n_heads, n_keys, _ = e1.shape
    E = peer_u.shape[0]
    tiles_per_seq = seq // tm
    n_i1 = te // n_keys
    assert n_i1 == SUBLANES, "expert rows of one block fill the sublanes of an f32 tile"
    kern = functools.partial(_expert_kernel, tm=tm, te=te, n_heads=n_heads, n_keys=n_keys)
    row_blk = pl.BlockSpec((n_heads, n_i1, tm), lambda i, j: (0, j, i))
    key_blk = pl.BlockSpec((n_heads, n_keys, tm), lambda i, j: (0, 0, i))
    return pl.pallas_call(
        kern,
        grid=(T // tm, E // te),
        in_specs=[
            pl.BlockSpec((tm, D), lambda i, j: (i, 0)),
            row_blk, key_blk,
            pl.BlockSpec((n_heads, tm), lambda i, j: (0, i)),
            pl.BlockSpec((te, D), lambda i, j: (j, 0)),
            pl.BlockSpec((None, D, te), lambda i, j: (j, 0, 0)),
            pl.BlockSpec((tm, D), lambda i, j: (i, 0)),
            pl.BlockSpec((1, 6, D), lambda i, j: (i // tiles_per_seq, 0, 0)),
            pl.BlockSpec((1, D), lambda i, j: (0, 0)),
        ],
        out_specs=pl.BlockSpec((tm, D), lambda i, j: (i, 0)),
        out_shape=jax.ShapeDtypeStruct((T, D), F32),
        scratch_shapes=[pltpu.VMEM((D, tm), F32),
                        pltpu.VMEM((te, tm), BF16)],
        compiler_params=_params(("arbitrary", "arbitrary")),
        name="peer_experts",
    )(h2, e1, e2, thr, peer_u, peer_vt, x1, mod3, final_g)


def kernel(x, c, positions, w_mod, b_mod, norm1_g, w_in, w_pool, pool_scale, w_out, norm2_g,
           w_query, sub_keys, peer_u, peer_v, final_g):
    B, S, D = x.shape
    assert w_mod.shape[0] == 1, "single-layer stack"
    assert S % ATTN_SPAN == 0
    tm = min(512, S)
    aw = w_in.shape[2] // 4
    x2d = x.reshape(B * S, D)

    mod3 = _modulation(c, w_mod[0], b_mod[0]).reshape(B, 6, D)
    qkv, pool = _in_projection(x2d, mod3, norm1_g[0].reshape(1, D), _rope_tables(positions),
                               w_in[0].astype(BF16), w_pool[0].astype(BF16),
                               pool_scale[0].reshape(1, -1), seq=S, tm=tm)
    attn = _attention(qkv, batch=B, seq=S, attn_width=aw)
    x1, h2 = _out_projection(attn, pool, x2d, mod3, norm2_g[0].reshape(1, D),
                             w_out[0].astype(BF16), seq=S, tm=tm)
    e1, e2, thr = _routing(h2, w_query[0].astype(BF16), sub_keys[0].astype(BF16), tm=tm)
    te = 1024
    n_exp = peer_v.shape[1]
    vt_blocks = peer_v[0].reshape(n_exp // te, te, D).transpose(0, 2, 1).astype(BF16)
    out = _experts(h2, e1, e2, thr, peer_u[0].astype(BF16), vt_blocks,
                   x1, mod3, final_g.reshape(1, D), seq=S, tm=tm, te=te)
    return out.reshape(B, S, D)
```

```python
import functools
import math

import jax
import jax.numpy as jnp
from jax import lax
from jax.experimental import pallas as pl
from jax.experimental.pallas import tpu as pltpu

F32 = jnp.float32
BF16 = jnp.bfloat16

HEAD_DIM = 128
ROPE_DIM = HEAD_DIM // 4
ROPE_HALF = ROPE_DIM // 2
ROPE_THETA = 500000.0
POOL_SIZES = (2, 4, 8, 16)
DILATIONS = (1, 4, 16)
ATTN_BLOCK = 128
TOPK = 16
NORM_EPS = 1e-6
NEG_BIG = -1e30

LANES = 128
SUBLANES = 8
VMEM_LIMIT = 56 * 1024 * 1024

ATTN_SPAN = ATTN_BLOCK * max(DILATIONS)
POOL_HALO = 16
EXPERT_SUB = 256


def _params(sem, vmem=VMEM_LIMIT):
    return pltpu.CompilerParams(dimension_semantics=sem, vmem_limit_bytes=vmem)


def _mod_kernel(c_ref, w_ref, b_ref, o_ref):
    c = c_ref[...]
    a = c * jax.nn.sigmoid(c)
    o_ref[...] = jnp.dot(a, w_ref[...], preferred_element_type=F32,
                         precision=lax.Precision.HIGHEST) + b_ref[...]


def _modulation(c, w_mod, b_mod):
    B, D = c.shape
    N = w_mod.shape[1]
    tn = 1024
    return pl.pallas_call(
        _mod_kernel,
        grid=(N // tn,),
        in_specs=[pl.BlockSpec((B, D), lambda j: (0, 0)),
                  pl.BlockSpec((D, tn), lambda j: (0, j)),
                  pl.BlockSpec((1, tn), lambda j: (0, j))],
        out_specs=pl.BlockSpec((B, tn), lambda j: (0, j)),
        out_shape=jax.ShapeDtypeStruct((B, N), F32),
        compiler_params=_params(("arbitrary",)),
        name="modulation",
    )(c, w_mod, b_mod.reshape(1, N))


def _rmsnorm_mod(x, g, shift, scale):
    ms = jnp.mean(x * x, axis=-1, keepdims=True)
    y = x * lax.rsqrt(ms + NORM_EPS) * g
    return y * (1.0 + scale) + shift


def _rope_kernel(pos_ref, inv_ref, cos_ref, sin_ref):
    ang = pos_ref[...].astype(F32) * inv_ref[...]
    cos_ref[...] = jnp.cos(ang)
    sin_ref[...] = jnp.sin(ang)


def _rope_tables(positions):
    T = positions.size
    per_row = LANES // ROPE_HALF
    pos_rep = jnp.repeat(positions.reshape(T // per_row, per_row), ROPE_HALF, axis=1)
    inv = ROPE_THETA ** (-jnp.arange(ROPE_HALF, dtype=F32) * 2.0 / ROPE_DIM)
    inv_rep = jnp.tile(inv, per_row).reshape(1, LANES)
    dense = pl.BlockSpec((T // per_row, LANES), lambda: (0, 0))
    cos_c, sin_c = pl.pallas_call(
        _rope_kernel,
        in_specs=[dense, pl.BlockSpec((1, LANES), lambda: (0, 0))],
        out_specs=[dense, dense],
        out_shape=[jax.ShapeDtypeStruct((T // per_row, LANES), F32)] * 2,
        name="rope_tables",
    )(pos_rep, inv_rep)
    cos_h, sin_h = cos_c.reshape(T, ROPE_HALF), sin_c.reshape(T, ROPE_HALF)
    zero_h = jnp.zeros_like(sin_h)
    rest = LANES - ROPE_DIM
    cos = jnp.concatenate([cos_h, cos_h, jnp.ones((T, rest), F32)], axis=1)
    sin_lo = jnp.concatenate([sin_h, zero_h, jnp.zeros((T, rest), F32)], axis=1)
    sin_hi = jnp.concatenate([zero_h, sin_h, jnp.zeros((T, rest), F32)], axis=1)
    return cos, sin_lo, sin_hi


def _inproj_kernel(x_ref, mod_ref, g_ref, cos_ref, sinlo_ref, sinhi_ref, w_ref, wpool_ref, pscale_ref,
                   qkv_ref, pool_ref, carry_scr, ext_scr,
                   *, tm, tiles_per_seq, attn_width):
    i = pl.program_id(0)
    aw = attn_width
    n_heads = aw // HEAD_DIM
    h = _rmsnorm_mod(x_ref[...], g_ref[...], mod_ref[0, 0:1, :], mod_ref[0, 1:2, :]).astype(BF16)

    def project(j):
        return jnp.dot(h, w_ref[:, j * aw:(j + 1) * aw], preferred_element_type=F32)

    def rotary(z, col0, scale):
        cos = cos_ref[...]
        sin_lo = sinlo_ref[...]
        sin_hi = sinhi_ref[...]
        for hh in range(n_heads):
            zh = z[:, hh * HEAD_DIM:(hh + 1) * HEAD_DIM]
            up = pltpu.roll(zh, HEAD_DIM - ROPE_HALF, axis=1)
            dn = pltpu.roll(zh, ROPE_HALF, axis=1)
            out = zh * cos - up * sin_lo + dn * sin_hi
            if scale is not None:
                out = out * scale
            qkv_ref[:, col0 + hh * HEAD_DIM:col0 + (hh + 1) * HEAD_DIM] = out

    rotary(project(0), 0, HEAD_DIM ** -0.5)
    rotary(project(1), aw, None)
    qkv_ref[:, 2 * aw:3 * aw] = project(2)

    u = project(3)
    first = (i % tiles_per_seq) == 0

    @pl.when(first)
    def _():
        carry_scr[...] = jnp.zeros_like(carry_scr)

    ext_scr[0:POOL_HALO, :] = carry_scr[...]
    ext_scr[POOL_HALO:POOL_HALO + tm, :] = u
    carry_scr[...] = u[tm - POOL_HALO:tm, :]
    gw = aw // len(POOL_SIZES)
    t_in_seq = (i % tiles_per_seq) * tm + lax.broadcasted_iota(jnp.int32, (tm, gw), 0)
    for gi, p in enumerate(POOL_SIZES):
        cols = slice(gi * gw, (gi + 1) * gw)
        u_g = u[:, cols]
        acc = u_g
        for back in range(1, p):
            acc = acc + ext_scr[POOL_HALO - back:POOL_HALO - back + tm, cols]
        cnt = jnp.minimum(t_in_seq + 1, p).astype(F32)
        r = acc / cnt - u_g
        y = jnp.dot(r.astype(BF16), wpool_ref[gi], preferred_element_type=F32)
        pool_ref[:, cols] = (y * pscale_ref[:, cols]).astype(pool_ref.dtype)


def _in_projection(x2d, mod3, norm_g, rope, w_in, w_pool, pool_scale, *, seq, tm):
    T, D = x2d.shape
    n_in = w_in.shape[1]
    aw = n_in // 4
    tiles_per_seq = seq // tm
    kern = functools.partial(_inproj_kernel, tm=tm, tiles_per_seq=tiles_per_seq, attn_width=aw)
    return pl.pallas_call(
        kern,
        grid=(T // tm,),
        in_specs=[
            pl.BlockSpec((tm, D), lambda i: (i, 0)),
            pl.BlockSpec((1, 6, D), lambda i: (i // tiles_per_seq, 0, 0)),
            pl.BlockSpec((1, D), lambda i: (0, 0)),
            pl.BlockSpec((tm, LANES), lambda i: (i, 0)),
            pl.BlockSpec((tm, LANES), lambda i: (i, 0)),
            pl.BlockSpec((tm, LANES), lambda i: (i, 0)),
            pl.BlockSpec((D, n_in), lambda i: (0, 0), pipeline_mode=pl.Buffered(1)),
            pl.BlockSpec(w_pool.shape, lambda i: (0, 0, 0)),
            pl.BlockSpec((1, aw), lambda i: (0, 0)),
        ],
        out_specs=[
            pl.BlockSpec((tm, 3 * aw), lambda i: (i, 0)),
            pl.BlockSpec((tm, aw), lambda i: (i, 0)),
        ],
        out_shape=[jax.ShapeDtypeStruct((T, 3 * aw), F32),
                   jax.ShapeDtypeStruct((T, aw), BF16)],
        scratch_shapes=[
            pltpu.VMEM((POOL_HALO, aw), F32),
            pltpu.VMEM((POOL_HALO + tm, aw), F32),
        ],
        compiler_params=_params(("arbitrary",)),
        name="in_projection",
    )(x2d, mod3, norm_g, *rope, w_in, w_pool, pool_scale)


def _strided(start, size, stride):
    return pl.ds(start, size) if stride == 1 else pl.ds(start, size, stride=stride)


def _attn_kernel(q_ref, kp_ref, kc_ref, vp_ref, vc_ref, o_ref, o_scr, l_scr, plane_scr, res_scr):
    n = pl.program_id(2)
    blk = ATTN_BLOCK
    qi = lax.broadcasted_iota(jnp.int32, (blk, 2 * blk), 0)
    kj = lax.broadcasted_iota(jnp.int32, (blk, 2 * blk), 1)
    dist = qi + blk - kj
    band = (dist >= 0) & (dist <= blk)
    band_first = band & ((kj >= blk) | (n > 0))

    def block(q, k_lo, k_hi, v_lo, v_hi, mask):
        k = jnp.concatenate([k_lo, k_hi], axis=0).astype(BF16)
        v = jnp.concatenate([v_lo, v_hi], axis=0).astype(BF16)
        s = lax.dot_general(q.astype(BF16), k, (((1,), (1,)), ((), ())),
                            preferred_element_type=F32)
        s = jnp.where(mask, s, NEG_BIG)
        m = jnp.max(s, axis=-1, keepdims=True)
        p = jnp.exp(s - m)
        den = jnp.sum(p, axis=-1, keepdims=True)
        o = jnp.dot(p.astype(BF16), v, preferred_element_type=F32) / den
        return o, jnp.broadcast_to(m + jnp.log(den), (blk, HEAD_DIM))

    for pi, d in enumerate(DILATIONS[:2]):
        for r in range(d):
            for c in range(ATTN_SPAN // (blk * d)):
                start = r + d * blk * c
                rows = _strided(start, blk, d)
                if c == 0:
                    lo_rows = _strided(ATTN_SPAN + r - d * blk, blk, d)
                    k_lo, v_lo = kp_ref[lo_rows, :], vp_ref[lo_rows, :]
                else:
                    lo_rows = _strided(start - d * blk, blk, d)
                    k_lo, v_lo = kc_ref[lo_rows, :], vc_ref[lo_rows, :]
                o, l = block(q_ref[rows, :], k_lo, kc_ref[rows, :], v_lo, vc_ref[rows, :],
                             band_first if c == 0 else band)
                o_scr[pi, rows, :] = o
                l_scr[pi, rows, :] = l

    sub, per_plane = 4, ATTN_SPAN // 4
    assert DILATIONS[2] == sub * sub and ATTN_SPAN == blk * DILATIONS[2]
    for rho in range(sub):
        plane = pl.ds(rho, per_plane, stride=sub)
        for t, src in enumerate((q_ref, kp_ref, kc_ref, vp_ref, vc_ref)):
            plane_scr[t] = src[plane, :]
        for c2 in range(sub):
            rows = pl.ds(c2, blk, stride=sub)
            o, l = block(plane_scr[0, rows, :], plane_scr[1, rows, :], plane_scr[2, rows, :],
                         plane_scr[3, rows, :], plane_scr[4, rows, :], band_first)
            res_scr[0, rows, :] = o
            res_scr[1, rows, :] = l
        o_scr[2, plane, :] = res_scr[0]
        l_scr[2, plane, :] = res_scr[1]

    l0, l1, l2 = l_scr[0], l_scr[1], l_scr[2]
    lmax = jnp.maximum(jnp.maximum(l0, l1), l2)
    e0, e1, e2 = jnp.exp(l0 - lmax), jnp.exp(l1 - lmax), jnp.exp(l2 - lmax)
    mixed = (e0 * o_scr[0] + e1 * o_scr[1] + e2 * o_scr[2]) / (e0 + e1 + e2)
    o_ref[...] = mixed.astype(o_ref.dtype)


def _attention(qkv, *, batch, seq, attn_width):
    T = qkv.shape[0]
    n_heads = attn_width // HEAD_DIM
    spans = seq // ATTN_SPAN
    blk = (ATTN_SPAN, HEAD_DIM)

    def cur(col0):
        return lambda b, h, n: (b * spans + n, col0 + h)

    def prev(col0):
        return lambda b, h, n: (b * spans + jnp.maximum(n - 1, 0), col0 + h)

    return pl.pallas_call(
        _attn_kernel,
        grid=(batch, n_heads, spans),
        in_specs=[pl.BlockSpec(blk, cur(0)),
                  pl.BlockSpec(blk, prev(n_heads)),
                  pl.BlockSpec(blk, cur(n_heads)),
                  pl.BlockSpec(blk, prev(2 * n_heads)),
                  pl.BlockSpec(blk, cur(2 * n_heads))],
        out_specs=pl.BlockSpec(blk, cur(0)),
        out_shape=jax.ShapeDtypeStruct((T, attn_width), BF16),
        scratch_shapes=[pltpu.VMEM((len(DILATIONS),) + blk, F32),
                        pltpu.VMEM((len(DILATIONS),) + blk, F32),
                        pltpu.VMEM((5, ATTN_SPAN // 4, HEAD_DIM), F32),
                        pltpu.VMEM((2, ATTN_SPAN // 4, HEAD_DIM), F32)],
        compiler_params=_params(("arbitrary", "arbitrary", "arbitrary")),
        name="dilated_attention",
    )(qkv, qkv, qkv, qkv, qkv)


def _outproj_kernel(attn_ref, pool_ref, x_ref, mod_ref, g_ref, w_ref, x1_ref, h2_ref, h2t_ref,
                    *, attn_width):
    mix = jnp.dot(attn_ref[...], w_ref[0:attn_width, :], preferred_element_type=F32)
    mix = mix + jnp.dot(pool_ref[...], w_ref[attn_width:, :], preferred_element_type=F32)
    x1 = x_ref[...] + mod_ref[0, 2:3, :] * mix
    x1_ref[...] = x1
    h2 = _rmsnorm_mod(x1, g_ref[...], mod_ref[0, 3:4, :], mod_ref[0, 4:5, :])
    h2_ref[...] = h2.astype(BF16)
    h2t_ref[...] = h2.T.astype(BF16)


def _out_projection(attn, pool, x2d, mod3, norm_g, w_out, *, seq, tm):
    T, D = x2d.shape
    aw = attn.shape[1]
    pw = pool.shape[1]
    tiles_per_seq = seq // tm
    return pl.pallas_call(
        functools.partial(_outproj_kernel, attn_width=aw),
        grid=(T // tm,),
        in_specs=[
            pl.BlockSpec((tm, aw), lambda i: (i, 0)),
            pl.BlockSpec((tm, pw), lambda i: (i, 0)),
            pl.BlockSpec((tm, D), lambda i: (i, 0)),
            pl.BlockSpec((1, 6, D), lambda i: (i // tiles_per_seq, 0, 0)),
            pl.BlockSpec((1, D), lambda i: (0, 0)),
            pl.BlockSpec((aw + pw, D), lambda i: (0, 0)),
        ],
        out_specs=[pl.BlockSpec((tm, D), lambda i: (i, 0)),
                   pl.BlockSpec((tm, D), lambda i: (i, 0)),
                   pl.BlockSpec((D, tm), lambda i: (0, i))],
        out_shape=[jax.ShapeDtypeStruct((T, D), F32),
                   jax.ShapeDtypeStruct((T, D), BF16),
                   jax.ShapeDtypeStruct((D, T), BF16)],
        compiler_params=_params(("arbitrary",)),
        name="out_projection",
    )(attn, pool, x2d, mod3, norm_g, w_out)


def _oddeven_merge_sort_pairs(n):
    pairs = []

    def merge(lo, hi, r):
        step = r * 2
        if step < hi - lo:
            merge(lo, hi, step)
            merge(lo + r, hi, step)
            for k in range(lo + r, hi - r, step):
                pairs.append((k, k + r))
        else:
            pairs.append((lo, lo + r))

    def sort(lo, hi):
        if hi - lo >= 1:
            mid = lo + (hi - lo) // 2
            sort(lo, mid)
            sort(mid + 1, hi)
            merge(lo, hi, 1)

    sort(0, n - 1)
    return pairs


_SORT16 = _oddeven_merge_sort_pairs(TOPK)


def _sort_desc(vals):
    vals = list(vals)
    for a, b in _SORT16:
        hi, lo = jnp.maximum(vals[a], vals[b]), jnp.minimum(vals[a], vals[b])
        vals[a], vals[b] = hi, lo
    return vals


def _merge_top(a_list, b_list):
    n = TOPK
    c = [jnp.maximum(a_list[k], b_list[n - 1 - k]) for k in range(n)]
    stride = n // 2
    while stride >= 1:
        for k in range(n):
            if (k & stride) == 0:
                hi, lo = jnp.maximum(c[k], c[k + stride]), jnp.minimum(c[k], c[k + stride])
                c[k], c[k + stride] = hi, lo
        stride //= 2
    return c


_CAND_ROWS = [[(i, j) for j in range(TOPK) if (i + 1) * (j + 1) <= TOPK] for i in range(TOPK)]


def _route_kernel(h_ref, wq_ref, keys_ref, e1_ref, e2_ref, thr_ref, qp_scr, top_scr, s2_scr,
                  *, tm, n_heads, n_keys):
    qp = jnp.dot(h_ref[...], wq_ref[...], preferred_element_type=F32)
    qp_scr[...] = qp.astype(BF16)
    n_chunks = tm // LANES
    groups = n_keys // SUBLANES
    assert groups == TOPK

    for h in range(n_heads):
        for half, out_ref in ((0, e1_ref), (1, s2_scr)):
            col0 = (2 * h + half) * n_keys
            s_t = lax.dot_general(keys_ref[h, half], qp_scr[:, col0:col0 + n_keys],
                                  (((1,), (1,)), ((), ())), preferred_element_type=F32)
            out_ref[h] = s_t
            for cidx in range(n_chunks):
                lanes = slice(cidx * LANES, (cidx + 1) * LANES)
                blk = s_t[:, lanes]
                vals = _sort_desc([blk[g * SUBLANES:(g + 1) * SUBLANES, :] for g in range(groups)])
                for shift in (4, 2, 1):
                    partner = [pltpu.roll(v, shift, axis=0) for v in vals]
                    vals = _merge_top(vals, partner)
                for k in range(TOPK):
                    top_scr[half, k, h:h + 1, lanes] = vals[k][0:1, :]

    for cidx in range(n_chunks):
        lanes = slice(cidx * LANES, (cidx + 1) * LANES)
        a = [top_scr[0, k, :, lanes] for k in range(TOPK)]
        b = [top_scr[1, k, :, lanes] for k in range(TOPK)]
        ea = [jnp.exp(v - a[0]) for v in a]
        eb = [jnp.exp(v - b[0]) for v in b]
        pad = jnp.full_like(a[0], -1.0)
        best = None
        for row in _CAND_ROWS:
            lst = [ea[i] * eb[j] for (i, j) in row]
            lst = lst + [pad] * (TOPK - len(lst))
            best = lst if best is None else _merge_top(best, lst)
        z = best[0]
        for k in range(1, TOPK):
            z = z + best[k]
        rz = 1.0 / z
        cut = best[TOPK - 1]
        ebn = [v * rz for v in eb]
        thr = None
        for row in _CAND_ROWS:
            for (i, j) in row:
                sel = ea[i] * eb[j] >= cut
                cand = jnp.where(sel, ea[i] * ebn[j], jnp.inf)
                thr = cand if thr is None else jnp.minimum(thr, cand)
        thr_ref[:, lanes] = thr
        top_scr[0, 0, :, lanes] = a[0]
        top_scr[1, 0, :, lanes] = b[0]
        top_scr[1, 1, :, lanes] = rz

    for h in range(n_heads):
        m1 = top_scr[0, 0, h:h + 1, :]
        m2 = top_scr[1, 0, h:h + 1, :]
        rz = top_scr[1, 1, h:h + 1, :]
        e1_ref[h] = jnp.exp(e1_ref[h] - m1)
        e2 = jnp.exp(s2_scr[h] - m2) * rz
        for cidx in range(n_chunks):
            e2_ref[h, cidx] = e2[:, cidx * LANES:(cidx + 1) * LANES]


def _routing(h2, w_query, sub_keys, *, tm):
    T, D = h2.shape
    n_heads, _, n_keys, kd = sub_keys.shape
    qw = w_query.shape[1]
    kern = functools.partial(_route_kernel, tm=tm, n_heads=n_heads, n_keys=n_keys)
    key_blk = pl.BlockSpec((n_heads, n_keys, tm), lambda i: (0, 0, i))
    return pl.pallas_call(
        kern,
        grid=(T // tm,),
        in_specs=[pl.BlockSpec((tm, D), lambda i: (i, 0)),
                  pl.BlockSpec((D, qw), lambda i: (0, 0)),
                  pl.BlockSpec(sub_keys.shape, lambda i: (0, 0, 0, 0))],
        out_specs=[key_blk,
                   pl.BlockSpec((n_heads, tm // LANES, n_keys, LANES), lambda i: (0, i, 0, 0)),
                   pl.BlockSpec((n_heads, tm), lambda i: (0, i))],
        out_shape=[jax.ShapeDtypeStruct((n_heads, n_keys, T), F32),
                   jax.ShapeDtypeStruct((n_heads, T // LANES, n_keys, LANES), F32),
                   jax.ShapeDtypeStruct((n_heads, T), F32)],
        scratch_shapes=[pltpu.VMEM((tm, qw), BF16),
                        pltpu.VMEM((2, TOPK, n_heads, tm), F32),
                        pltpu.VMEM((n_heads, n_keys, tm), F32)],
        compiler_params=_params(("arbitrary",)),
        name="peer_routing",
    )(h2, w_query, sub_keys)


def _gelu(a):
    return a * (lax.erf(a * (1.0 / math.sqrt(2.0))) + 1.0) * 0.5


def _expert_kernel(ht_ref, e1_ref, e2_ref, thr_ref, u_ref, vt_ref, x1_ref, mod_ref, g_ref,
                   o_ref, acc_scr, act_scr, pre_scr, *, tm, te, n_heads, n_keys):
    j = pl.program_id(1)
    n_i1 = te // n_keys
    n_chunks = tm // LANES
    n_groups = n_keys // SUBLANES

    @pl.when(j == 0)
    def _():
        acc_scr[...] = jnp.zeros_like(acc_scr)

    for s in range(te // EXPERT_SUB):
        rows = slice(s * EXPERT_SUB, (s + 1) * EXPERT_SUB)
        pre_scr[rows, :] = jnp.dot(u_ref[rows, :], ht_ref[...], preferred_element_type=F32)

    def activation(k, cidx):
        lanes = slice(cidx * LANES, (cidx + 1) * LANES)
        g = _gelu(pre_scr[k * n_keys:(k + 1) * n_keys, lanes])
        tie = g[0:SUBLANES, :] * 0.0
        gate = [None] * n_groups
        for h in range(n_heads):
            e1_rep = e1_ref[h, k:k + 1, lanes] + tie
            thr_row = thr_ref[h:h + 1, lanes]
            for r in range(n_groups):
                p = e1_rep * e2_ref[h, cidx, r * SUBLANES:(r + 1) * SUBLANES, :]
                sel = jnp.where(p >= thr_row, p, 0.0)
                gate[r] = sel if gate[r] is None else gate[r] + sel
        gate = jnp.concatenate(gate, axis=0)
        act_scr[k * n_keys:(k + 1) * n_keys, lanes] = (g * gate).astype(BF16)

    for c in range(2):
        for k in range(n_i1):
            for cidx in range(c * n_chunks // 2, (c + 1) * n_chunks // 2):
                activation(k, cidx)
        cols = slice(c * tm // 2, (c + 1) * tm // 2)
        acc_scr[:, cols] += jnp.dot(vt_ref[...], act_scr[:, cols], preferred_element_type=F32)

    @pl.when(j == pl.num_programs(1) - 1)
    def _():
        y = acc_scr[...].T
        x2 = x1_ref[...] + mod_ref[0, 5:6, :] * y
        ms = jnp.mean(x2 * x2, axis=-1, keepdims=True)
        o_ref[...] = x2 * lax.rsqrt(ms + NORM_EPS) * g_ref[...]


def _experts(h2t, e1, e2, thr, peer_u, peer_vt, x1, mod3, final_g, *, seq, tm, te):
    D, T = h2t.shape
    n_heads, n_keys, _ = e1.shape
    E = peer_u.shape[0]
    tiles_per_seq = seq // tm
    n_i1 = te // n_keys
    assert n_i1 == SUBLANES, "expert rows of one block fill the sublanes of an f32 tile"
    kern = functools.partial(_expert_kernel, tm=tm, te=te, n_heads=n_heads, n_keys=n_keys)
    row_blk = pl.BlockSpec((n_heads, n_i1, tm), lambda i, j: (0, j, i))
    key_blk = pl.BlockSpec((n_heads, tm // LANES, n_keys, LANES), lambda i, j: (0, i, 0, 0))
    return pl.pallas_call(
        kern,
        grid=(T // tm, E // te),
        in_specs=[
            pl.BlockSpec((D, tm), lambda i, j: (0, i)),
            row_blk, key_blk,
            pl.BlockSpec((n_heads, tm), lambda i, j: (0, i)),
            pl.BlockSpec((te, D), lambda i, j: (j, 0)),
            pl.BlockSpec((None, D, te), lambda i, j: (j, 0, 0)),
            pl.BlockSpec((tm, D), lambda i, j: (i, 0)),
            pl.BlockSpec((1, 6, D), lambda i, j: (i // tiles_per_seq, 0, 0)),
            pl.BlockSpec((1, D), lambda i, j: (0, 0)),
        ],
        out_specs=pl.BlockSpec((tm, D), lambda i, j: (i, 0)),
        out_shape=jax.ShapeDtypeStruct((T, D), F32),
        scratch_shapes=[pltpu.VMEM((D, tm), F32),
                        pltpu.VMEM((te, tm), BF16),
                        pltpu.VMEM((te, tm), F32)],
        compiler_params=_params(("arbitrary", "arbitrary")),
        name="peer_experts",
    )(h2t, e1, e2, thr, peer_u, peer_vt, x1, mod3, final_g)


def kernel(x, c, positions, w_mod, b_mod, norm1_g, w_in, w_pool, pool_scale, w_out, norm2_g,
           w_query, sub_keys, peer_u, peer_v, final_g):
    B, S, D = x.shape
    assert w_mod.shape[0] == 1, "single-layer stack"
    assert S % ATTN_SPAN == 0
    tm = min(512, S)
    aw = w_in.shape[2] // 4
    x2d = x.reshape(B * S, D)

    mod3 = _modulation(c, w_mod[0], b_mod[0]).reshape(B, 6, D)
    qkv, pool = _in_projection(x2d, mod3, norm1_g[0].reshape(1, D), _rope_tables(positions),
                               w_in[0].astype(BF16), w_pool[0].astype(BF16),
                               pool_scale[0].reshape(1, -1), seq=S, tm=tm)
    attn = _attention(qkv, batch=B, seq=S, attn_width=aw)
    x1, h2, h2t = _out_projection(attn, pool, x2d, mod3, norm2_g[0].reshape(1, D),
                                  w_out[0].astype(BF16), seq=S, tm=tm)
    e1, e2, thr = _routing(h2, w_query[0].astype(BF16), sub_keys[0].astype(BF16), tm=tm)
    te = 1024
    n_exp = peer_v.shape[1]
    vt_blocks = peer_v[0].reshape(n_exp // te, te, D).transpose(0, 2, 1).astype(BF16)
    out = _experts(h2t, e1, e2, thr, peer_u[0].astype(BF16), vt_blocks,
                   x1, mod3, final_g.reshape(1, D), seq=S, tm=tm, te=te)
    return out.reshape(B, S, D)
```

```python
import functools
import math

import jax
import jax.numpy as jnp
from jax import lax
from jax.experimental import pallas as pl
from jax.experimental.pallas import tpu as pltpu

F32 = jnp.float32
BF16 = jnp.bfloat16

HEAD_DIM = 128
ROPE_DIM = HEAD_DIM // 4
ROPE_HALF = ROPE_DIM // 2
ROPE_THETA = 500000.0
POOL_SIZES = (2, 4, 8, 16)
DILATIONS = (1, 4, 16)
ATTN_BLOCK = 128
TOPK = 16
NORM_EPS = 1e-6
NEG_BIG = -1e30

LANES = 128
SUBLANES = 8
VMEM_LIMIT = 56 * 1024 * 1024

ATTN_SPAN = ATTN_BLOCK * max(DILATIONS)
POOL_HALO = 16
EXPERT_SUB = 512


def _params(sem, vmem=VMEM_LIMIT):
    return pltpu.CompilerParams(dimension_semantics=sem, vmem_limit_bytes=vmem)


def _mod_kernel(c_ref, w_ref, b_ref, o_ref):
    c = c_ref[...]
    a = c * jax.nn.sigmoid(c)
    o_ref[...] = jnp.dot(a, w_ref[...], preferred_element_type=F32,
                         precision=lax.Precision.HIGHEST) + b_ref[...]


def _modulation(c, w_mod, b_mod):
    B, D = c.shape
    N = w_mod.shape[1]
    tn = 1024
    return pl.pallas_call(
        _mod_kernel,
        grid=(N // tn,),
        in_specs=[pl.BlockSpec((B, D), lambda j: (0, 0)),
                  pl.BlockSpec((D, tn), lambda j: (0, j)),
                  pl.BlockSpec((1, tn), lambda j: (0, j))],
        out_specs=pl.BlockSpec((B, tn), lambda j: (0, j)),
        out_shape=jax.ShapeDtypeStruct((B, N), F32),
        compiler_params=_params(("arbitrary",)),
        name="modulation",
    )(c, w_mod, b_mod.reshape(1, N))


def _rmsnorm_mod(x, g, shift, scale):
    ms = jnp.mean(x * x, axis=-1, keepdims=True)
    y = x * lax.rsqrt(ms + NORM_EPS) * g
    return y * (1.0 + scale) + shift


def _rope_kernel(pos_ref, inv_ref, cos_ref, sin_ref):
    ang = pos_ref[...].astype(F32) * inv_ref[...]
    cos_ref[...] = jnp.cos(ang)
    sin_ref[...] = jnp.sin(ang)


def _rope_tables(positions):
    T = positions.size
    per_row = LANES // ROPE_HALF
    pos_rep = jnp.repeat(positions.reshape(T // per_row, per_row), ROPE_HALF, axis=1)
    inv = ROPE_THETA ** (-jnp.arange(ROPE_HALF, dtype=F32) * 2.0 / ROPE_DIM)
    inv_rep = jnp.tile(inv, per_row).reshape(1, LANES)
    dense = pl.BlockSpec((T // per_row, LANES), lambda: (0, 0))
    cos_c, sin_c = pl.pallas_call(
        _rope_kernel,
        in_specs=[dense, pl.BlockSpec((1, LANES), lambda: (0, 0))],
        out_specs=[dense, dense],
        out_shape=[jax.ShapeDtypeStruct((T // per_row, LANES), F32)] * 2,
        name="rope_tables",
    )(pos_rep, inv_rep)
    cos_h, sin_h = cos_c.reshape(T, ROPE_HALF), sin_c.reshape(T, ROPE_HALF)
    zero_h = jnp.zeros_like(sin_h)
    rest = LANES - ROPE_DIM
    cos = jnp.concatenate([cos_h, cos_h, jnp.ones((T, rest), F32)], axis=1)
    sin_lo = jnp.concatenate([sin_h, zero_h, jnp.zeros((T, rest), F32)], axis=1)
    sin_hi = jnp.concatenate([zero_h, sin_h, jnp.zeros((T, rest), F32)], axis=1)
    return cos, sin_lo, sin_hi


def _inproj_kernel(x_ref, mod_ref, g_ref, cos_ref, sinlo_ref, sinhi_ref, w_ref, wpool_ref, pscale_ref,
                   qkv_ref, pool_ref, carry_scr, ext_scr,
                   *, tm, tiles_per_seq, attn_width):
    i = pl.program_id(0)
    aw = attn_width
    n_heads = aw // HEAD_DIM
    h = _rmsnorm_mod(x_ref[...], g_ref[...], mod_ref[0, 0:1, :], mod_ref[0, 1:2, :]).astype(BF16)

    def project(j):
        return jnp.dot(h, w_ref[:, j * aw:(j + 1) * aw], preferred_element_type=F32)

    def rotary(z, col0, scale):
        cos = cos_ref[...]
        sin_lo = sinlo_ref[...]
        sin_hi = sinhi_ref[...]
        for hh in range(n_heads):
            zh = z[:, hh * HEAD_DIM:(hh + 1) * HEAD_DIM]
            up = pltpu.roll(zh, HEAD_DIM - ROPE_HALF, axis=1)
            dn = pltpu.roll(zh, ROPE_HALF, axis=1)
            out = zh * cos - up * sin_lo + dn * sin_hi
            if scale is not None:
                out = out * scale
            qkv_ref[:, col0 + hh * HEAD_DIM:col0 + (hh + 1) * HEAD_DIM] = out

    rotary(project(0), 0, HEAD_DIM ** -0.5)
    rotary(project(1), aw, None)
    qkv_ref[:, 2 * aw:3 * aw] = project(2)

    u = project(3)
    first = (i % tiles_per_seq) == 0

    @pl.when(first)
    def _():
        carry_scr[...] = jnp.zeros_like(carry_scr)

    ext_scr[0:POOL_HALO, :] = carry_scr[...]
    ext_scr[POOL_HALO:POOL_HALO + tm, :] = u
    carry_scr[...] = u[tm - POOL_HALO:tm, :]
    gw = aw // len(POOL_SIZES)
    t_in_seq = (i % tiles_per_seq) * tm + lax.broadcasted_iota(jnp.int32, (tm, gw), 0)
    for gi, p in enumerate(POOL_SIZES):
        cols = slice(gi * gw, (gi + 1) * gw)
        u_g = u[:, cols]
        acc = u_g
        for back in range(1, p):
            acc = acc + ext_scr[POOL_HALO - back:POOL_HALO - back + tm, cols]
        cnt = jnp.minimum(t_in_seq + 1, p).astype(F32)
        r = acc / cnt - u_g
        y = jnp.dot(r.astype(BF16), wpool_ref[gi], preferred_element_type=F32)
        pool_ref[:, cols] = (y * pscale_ref[:, cols]).astype(pool_ref.dtype)


def _in_projection(x2d, mod3, norm_g, rope, w_in, w_pool, pool_scale, *, seq, tm):
    T, D = x2d.shape
    n_in = w_in.shape[1]
    aw = n_in // 4
    tiles_per_seq = seq // tm
    kern = functools.partial(_inproj_kernel, tm=tm, tiles_per_seq=tiles_per_seq, attn_width=aw)
    return pl.pallas_call(
        kern,
        grid=(T // tm,),
        in_specs=[
            pl.BlockSpec((tm, D), lambda i: (i, 0)),
            pl.BlockSpec((1, 6, D), lambda i: (i // tiles_per_seq, 0, 0)),
            pl.BlockSpec((1, D), lambda i: (0, 0)),
            pl.BlockSpec((tm, LANES), lambda i: (i, 0)),
            pl.BlockSpec((tm, LANES), lambda i: (i, 0)),
            pl.BlockSpec((tm, LANES), lambda i: (i, 0)),
            pl.BlockSpec((D, n_in), lambda i: (0, 0), pipeline_mode=pl.Buffered(1)),
            pl.BlockSpec(w_pool.shape, lambda i: (0, 0, 0)),
            pl.BlockSpec((1, aw), lambda i: (0, 0)),
        ],
        out_specs=[
            pl.BlockSpec((tm, 3 * aw), lambda i: (i, 0)),
            pl.BlockSpec((tm, aw), lambda i: (i, 0)),
        ],
        out_shape=[jax.ShapeDtypeStruct((T, 3 * aw), F32),
                   jax.ShapeDtypeStruct((T, aw), BF16)],
        scratch_shapes=[
            pltpu.VMEM((POOL_HALO, aw), F32),
            pltpu.VMEM((POOL_HALO + tm, aw), F32),
        ],
        compiler_params=_params(("arbitrary",)),
        name="in_projection",
    )(x2d, mod3, norm_g, *rope, w_in, w_pool, pool_scale)


def _strided(start, size, stride):
    return pl.ds(start, size) if stride == 1 else pl.ds(start, size, stride=stride)


def _attn_kernel(q_ref, kp_ref, kc_ref, vp_ref, vc_ref, o_ref, o_scr, l_scr, plane_scr, res_scr):
    n = pl.program_id(2)
    blk = ATTN_BLOCK
    qi = lax.broadcasted_iota(jnp.int32, (blk, 2 * blk), 0)
    kj = lax.broadcasted_iota(jnp.int32, (blk, 2 * blk), 1)
    dist = qi + blk - kj
    band = (dist >= 0) & (dist <= blk)
    band_first = band & ((kj >= blk) | (n > 0))

    def block(q, k_lo, k_hi, v_lo, v_hi, mask):
        k = jnp.concatenate([k_lo, k_hi], axis=0).astype(BF16)
        v = jnp.concatenate([v_lo, v_hi], axis=0).astype(BF16)
        s = lax.dot_general(q.astype(BF16), k, (((1,), (1,)), ((), ())),
                            preferred_element_type=F32)
        s = jnp.where(mask, s, NEG_BIG)
        m = jnp.max(s, axis=-1, keepdims=True)
        p = jnp.exp(s - m)
        den = jnp.sum(p, axis=-1, keepdims=True)
        o = jnp.dot(p.astype(BF16), v, preferred_element_type=F32) / den
        return o, jnp.broadcast_to(m + jnp.log(den), (blk, HEAD_DIM))

    for pi, d in enumerate(DILATIONS[:2]):
        for r in range(d):
            for c in range(ATTN_SPAN // (blk * d)):
                start = r + d * blk * c
                rows = _strided(start, blk, d)
                if c == 0:
                    lo_rows = _strided(ATTN_SPAN + r - d * blk, blk, d)
                    k_lo, v_lo = kp_ref[lo_rows, :], vp_ref[lo_rows, :]
                else:
                    lo_rows = _strided(start - d * blk, blk, d)
                    k_lo, v_lo = kc_ref[lo_rows, :], vc_ref[lo_rows, :]
                o, l = block(q_ref[rows, :], k_lo, kc_ref[rows, :], v_lo, vc_ref[rows, :],
                             band_first if c == 0 else band)
                o_scr[pi, rows, :] = o
                l_scr[pi, rows, :] = l

    sub, per_plane = 4, ATTN_SPAN // 4
    assert DILATIONS[2] == sub * sub and ATTN_SPAN == blk * DILATIONS[2]
    for rho in range(sub):
        plane = pl.ds(rho, per_plane, stride=sub)
        for t, src in enumerate((q_ref, kp_ref, kc_ref, vp_ref, vc_ref)):
            plane_scr[t] = src[plane, :]
        for c2 in range(sub):
            rows = pl.ds(c2, blk, stride=sub)
            o, l = block(plane_scr[0, rows, :], plane_scr[1, rows, :], plane_scr[2, rows, :],
                         plane_scr[3, rows, :], plane_scr[4, rows, :], band_first)
            res_scr[0, rows, :] = o
            res_scr[1, rows, :] = l
        o_scr[2, plane, :] = res_scr[0]
        l_scr[2, plane, :] = res_scr[1]

    l0, l1, l2 = l_scr[0], l_scr[1], l_scr[2]
    lmax = jnp.maximum(jnp.maximum(l0, l1), l2)
    e0, e1, e2 = jnp.exp(l0 - lmax), jnp.exp(l1 - lmax), jnp.exp(l2 - lmax)
    mixed = (e0 * o_scr[0] + e1 * o_scr[1] + e2 * o_scr[2]) / (e0 + e1 + e2)
    o_ref[...] = mixed.astype(o_ref.dtype)


def _attention(qkv, *, batch, seq, attn_width):
    T = qkv.shape[0]
    n_heads = attn_width // HEAD_DIM
    spans = seq // ATTN_SPAN
    blk = (ATTN_SPAN, HEAD_DIM)

    def cur(col0):
        return lambda b, h, n: (b * spans + n, col0 + h)

    def prev(col0):
        return lambda b, h, n: (b * spans + jnp.maximum(n - 1, 0), col0 + h)

    return pl.pallas_call(
        _attn_kernel,
        grid=(batch, n_heads, spans),
        in_specs=[pl.BlockSpec(blk, cur(0)),
                  pl.BlockSpec(blk, prev(n_heads)),
                  pl.BlockSpec(blk, cur(n_heads)),
                  pl.BlockSpec(blk, prev(2 * n_heads)),
                  pl.BlockSpec(blk, cur(2 * n_heads))],
        out_specs=pl.BlockSpec(blk, cur(0)),
        out_shape=jax.ShapeDtypeStruct((T, attn_width), BF16),
        scratch_shapes=[pltpu.VMEM((len(DILATIONS),) + blk, F32),
                        pltpu.VMEM((len(DILATIONS),) + blk, F32),
                        pltpu.VMEM((5, ATTN_SPAN // 4, HEAD_DIM), F32),
                        pltpu.VMEM((2, ATTN_SPAN // 4, HEAD_DIM), F32)],
        compiler_params=_params(("arbitrary", "arbitrary", "arbitrary")),
        name="dilated_attention",
    )(qkv, qkv, qkv, qkv, qkv)


def _outproj_kernel(attn_ref, pool_ref, x_ref, mod_ref, g_ref, w_ref, x1_ref, h2_ref, *, attn_width):
    mix = jnp.dot(attn_ref[...], w_ref[0:attn_width, :], preferred_element_type=F32)
    mix = mix + jnp.dot(pool_ref[...], w_ref[attn_width:, :], preferred_element_type=F32)
    x1 = x_ref[...] + mod_ref[0, 2:3, :] * mix
    x1_ref[...] = x1
    h2 = _rmsnorm_mod(x1, g_ref[...], mod_ref[0, 3:4, :], mod_ref[0, 4:5, :])
    h2_ref[...] = h2.astype(BF16)


def _out_projection(attn, pool, x2d, mod3, norm_g, w_out, *, seq, tm):
    T, D = x2d.shape
    aw = attn.shape[1]
    pw = pool.shape[1]
    tiles_per_seq = seq // tm
    return pl.pallas_call(
        functools.partial(_outproj_kernel, attn_width=aw),
        grid=(T // tm,),
        in_specs=[
            pl.BlockSpec((tm, aw), lambda i: (i, 0)),
            pl.BlockSpec((tm, pw), lambda i: (i, 0)),
            pl.BlockSpec((tm, D), lambda i: (i, 0)),
            pl.BlockSpec((1, 6, D), lambda i: (i // tiles_per_seq, 0, 0)),
            pl.BlockSpec((1, D), lambda i: (0, 0)),
            pl.BlockSpec((aw + pw, D), lambda i: (0, 0)),
        ],
        out_specs=[pl.BlockSpec((tm, D), lambda i: (i, 0)),
                   pl.BlockSpec((tm, D), lambda i: (i, 0))],
        out_shape=[jax.ShapeDtypeStruct((T, D), F32),
                   jax.ShapeDtypeStruct((T, D), BF16)],
        compiler_params=_params(("arbitrary",)),
        name="out_projection",
    )(attn, pool, x2d, mod3, norm_g, w_out)


def _oddeven_merge_sort_pairs(n):
    pairs = []

    def merge(lo, hi, r):
        step = r * 2
        if step < hi - lo:
            merge(lo, hi, step)
            merge(lo + r, hi, step)
            for k in range(lo + r, hi - r, step):
                pairs.append((k, k + r))
        else:
            pairs.append((lo, lo + r))

    def sort(lo, hi):
        if hi - lo >= 1:
            mid = lo + (hi - lo) // 2
            sort(lo, mid)
            sort(mid + 1, hi)
            merge(lo, hi, 1)

    sort(0, n - 1)
    return pairs


_SORT16 = _oddeven_merge_sort_pairs(TOPK)


def _sort_desc(vals):
    vals = list(vals)
    for a, b in _SORT16:
        hi, lo = jnp.maximum(vals[a], vals[b]), jnp.minimum(vals[a], vals[b])
        vals[a], vals[b] = hi, lo
    return vals


def _merge_top(a_list, b_list):
    n = TOPK
    c = [jnp.maximum(a_list[k], b_list[n - 1 - k]) for k in range(n)]
    stride = n // 2
    while stride >= 1:
        for k in range(n):
            if (k & stride) == 0:
                hi, lo = jnp.maximum(c[k], c[k + stride]), jnp.minimum(c[k], c[k + stride])
                c[k], c[k + stride] = hi, lo
        stride //= 2
    return c


_CAND_ROWS = [[(i, j) for j in range(TOPK) if (i + 1) * (j + 1) <= TOPK] for i in range(TOPK)]


def _route_kernel(h_ref, wq_ref, keys_ref, e1_ref, e2_ref, thr_ref, qp_scr, top_scr,
                  *, tm, n_heads, n_keys):
    qp = jnp.dot(h_ref[...], wq_ref[...], preferred_element_type=F32)
    qp_scr[...] = qp.astype(BF16)
    n_chunks = tm // LANES
    groups = n_keys // SUBLANES
    assert groups == TOPK

    for h in range(n_heads):
        for half, out_ref in ((0, e1_ref), (1, e2_ref)):
            col0 = (2 * h + half) * n_keys
            s_t = lax.dot_general(keys_ref[h, half], qp_scr[:, col0:col0 + n_keys],
                                  (((1,), (1,)), ((), ())), preferred_element_type=F32)
            out_ref[h] = s_t
            for cidx in range(n_chunks):
                lanes = slice(cidx * LANES, (cidx + 1) * LANES)
                blk = s_t[:, lanes]
                vals = _sort_desc([blk[g * SUBLANES:(g + 1) * SUBLANES, :] for g in range(groups)])
                for shift in (4, 2, 1):
                    partner = [pltpu.roll(v, shift, axis=0) for v in vals]
                    vals = _merge_top(vals, partner)
                for k in range(TOPK):
                    top_scr[half, k, h:h + 1, lanes] = vals[k][0:1, :]

    for cidx in range(n_chunks):
        lanes = slice(cidx * LANES, (cidx + 1) * LANES)
        a = [top_scr[0, k, :, lanes] for k in range(TOPK)]
        b = [top_scr[1, k, :, lanes] for k in range(TOPK)]
        ea = [jnp.exp(v - a[0]) for v in a]
        eb = [jnp.exp(v - b[0]) for v in b]
        pad = jnp.full_like(a[0], -1.0)
        best = None
        for row in _CAND_ROWS:
            lst = [ea[i] * eb[j] for (i, j) in row]
            lst = lst + [pad] * (TOPK - len(lst))
            best = lst if best is None else _merge_top(best, lst)
        z = best[0]
        for k in range(1, TOPK):
            z = z + best[k]
        rz = 1.0 / z
        cut = best[TOPK - 1]
        ebn = [v * rz for v in eb]
        thr = None
        for row in _CAND_ROWS:
            for (i, j) in row:
                sel = ea[i] * eb[j] >= cut
                cand = jnp.where(sel, ea[i] * ebn[j], jnp.inf)
                thr = cand if thr is None else jnp.minimum(thr, cand)
        thr_ref[:, lanes] = thr
        top_scr[0, 0, :, lanes] = a[0]
        top_scr[1, 0, :, lanes] = b[0]
        top_scr[1, 1, :, lanes] = rz

    for h in range(n_heads):
        m1 = top_scr[0, 0, h:h + 1, :]
        m2 = top_scr[1, 0, h:h + 1, :]
        rz = top_scr[1, 1, h:h + 1, :]
        e1_ref[h] = jnp.exp(e1_ref[h] - m1)
        e2_ref[h] = jnp.exp(e2_ref[h] - m2) * rz


def _routing(h2, w_query, sub_keys, *, tm):
    T, D = h2.shape
    n_heads, _, n_keys, kd = sub_keys.shape
    qw = w_query.shape[1]
    kern = functools.partial(_route_kernel, tm=tm, n_heads=n_heads, n_keys=n_keys)
    key_blk = pl.BlockSpec((n_heads, n_keys, tm), lambda i: (0, 0, i))
    return pl.pallas_call(
        kern,
        grid=(T // tm,),
        in_specs=[pl.BlockSpec((tm, D), lambda i: (i, 0)),
                  pl.BlockSpec((D, qw), lambda i: (0, 0)),
                  pl.BlockSpec(sub_keys.shape, lambda i: (0, 0, 0, 0))],
        out_specs=[key_blk, key_blk, pl.BlockSpec((n_heads, tm), lambda i: (0, i))],
        out_shape=[jax.ShapeDtypeStruct((n_heads, n_keys, T), F32),
                   jax.ShapeDtypeStruct((n_heads, n_keys, T), F32),
                   jax.ShapeDtypeStruct((n_heads, T), F32)],
        scratch_shapes=[pltpu.VMEM((tm, qw), BF16),
                        pltpu.VMEM((2, TOPK, n_heads, tm), F32)],
        compiler_params=_params(("arbitrary",)),
        name="peer_routing",
    )(h2, w_query, sub_keys)


def _gelu(a):
    return a * (lax.erf(a * (1.0 / math.sqrt(2.0))) + 1.0) * 0.5


def _expert_kernel(h_ref, e1_ref, e2_ref, thr_ref, u_ref, vt_ref, x1_ref, mod_ref, g_ref,
                   o_ref, acc_scr, act_scr, pre_scr, *, tm, te, n_heads, n_keys):
    j = pl.program_id(1)
    n_i1 = te // n_keys
    n_chunks = tm // LANES

    @pl.when(j == 0)
    def _():
        acc_scr[...] = jnp.zeros_like(acc_scr)

    n_groups = n_keys // SUBLANES

    for s in range(te // EXPERT_SUB):
        rows = slice(s * EXPERT_SUB, (s + 1) * EXPERT_SUB)
        pre_scr[rows, :] = lax.dot_general(u_ref[rows, :], h_ref[...], (((1,), (1,)), ((), ())),
                                           preferred_element_type=F32)

    def activation(k, cidx):
        lanes = slice(cidx * LANES, (cidx + 1) * LANES)
        g = _gelu(pre_scr[k * n_keys:(k + 1) * n_keys, lanes])
        tie = g[0:SUBLANES, :] * 0.0
        gate = [None] * n_groups
        for h in range(n_heads):
            e1_rep = e1_ref[h, k:k + 1, lanes] + tie
            thr_row = thr_ref[h:h + 1, lanes]
            for r in range(n_groups):
                p = e1_rep * e2_ref[h, r * SUBLANES:(r + 1) * SUBLANES, lanes]
                sel = jnp.where(p >= thr_row, p, 0.0)
                gate[r] = sel if gate[r] is None else gate[r] + sel
        gate = jnp.concatenate(gate, axis=0)
        act_scr[k * n_keys:(k + 1) * n_keys, lanes] = (g * gate).astype(BF16)

    for c in range(2):
        for k in range(n_i1):
            for cidx in range(c * n_chunks // 2, (c + 1) * n_chunks // 2):
                activation(k, cidx)
        cols = slice(c * tm // 2, (c + 1) * tm // 2)
        acc_scr[:, cols] += jnp.dot(vt_ref[...], act_scr[:, cols], preferred_element_type=F32)

    @pl.when(j == pl.num_programs(1) - 1)
    def _():
        y = acc_scr[...].T
        x2 = x1_ref[...] + mod_ref[0, 5:6, :] * y
        ms = jnp.mean(x2 * x2, axis=-1, keepdims=True)
        o_ref[...] = x2 * lax.rsqrt(ms + NORM_EPS) * g_ref[...]


def _experts(h2, e1, e2, thr, peer_u, peer_vt, x1, mod3, final_g, *, seq, tm, te):
    T, D = h2.shape
    n_heads, n_keys, _ = e1.shape
    E = peer_u.shape[0]
    tiles_per_seq = seq // tm
    n_i1 = te // n_keys
    assert n_i1 == SUBLANES, "expert rows of one block fill the sublanes of an f32 tile"
    kern = functools.partial(_expert_kernel, tm=tm, te=te, n_heads=n_heads, n_keys=n_keys)
    row_blk = pl.BlockSpec((n_heads, n_i1, tm), lambda i, j: (0, j, i))
    key_blk = pl.BlockSpec((n_heads, n_keys, tm), lambda i, j: (0, 0, i))
    return pl.pallas_call(
        kern,
        grid=(T // tm, E // te),
        in_specs=[
            pl.BlockSpec((tm, D), lambda i, j: (i, 0)),
            row_blk, key_blk,
            pl.BlockSpec((n_heads, tm), lambda i, j: (0, i)),
            pl.BlockSpec((te, D), lambda i, j: (j, 0)),
            pl.BlockSpec((None, D, te), lambda i, j: (j, 0, 0)),
            pl.BlockSpec((tm, D), lambda i, j: (i, 0)),
            pl.BlockSpec((1, 6, D), lambda i, j: (i // tiles_per_seq, 0, 0)),
            pl.BlockSpec((1, D), lambda i, j: (0, 0)),
        ],
        out_specs=pl.BlockSpec((tm, D), lambda i, j: (i, 0)),
        out_shape=jax.ShapeDtypeStruct((T, D), F32),
        scratch_shapes=[pltpu.VMEM((D, tm), F32),
                        pltpu.VMEM((te, tm), BF16),
                        pltpu.VMEM((te, tm), F32)],
        compiler_params=_params(("arbitrary", "arbitrary")),
        name="peer_experts",
    )(h2, e1, e2, thr, peer_u, peer_vt, x1, mod3, final_g)


def kernel(x, c, positions, w_mod, b_mod, norm1_g, w_in, w_pool, pool_scale, w_out, norm2_g,
           w_query, sub_keys, peer_u, peer_v, final_g):
    B, S, D = x.shape
    assert w_mod.shape[0] == 1, "single-layer stack"
    assert S % ATTN_SPAN == 0
    tm = min(512, S)
    aw = w_in.shape[2] // 4
    x2d = x.reshape(B * S, D)

    mod3 = _modulation(c, w_mod[0], b_mod[0]).reshape(B, 6, D)
    qkv, pool = _in_projection(x2d, mod3, norm1_g[0].reshape(1, D), _rope_tables(positions),
                               w_in[0].astype(BF16), w_pool[0].astype(BF16),
                               pool_scale[0].reshape(1, -1), seq=S, tm=tm)
    attn = _attention(qkv, batch=B, seq=S, attn_width=aw)
    x1, h2 = _out_projection(attn, pool, x2d, mod3, norm2_g[0].reshape(1, D),
                             w_out[0].astype(BF16), seq=S, tm=tm)
    e1, e2, thr = _routing(h2, w_query[0].astype(BF16), sub_keys[0].astype(BF16), tm=tm)
    te = 1024
    n_exp = peer_v.shape[1]
    vt_blocks = peer_v[0].reshape(n_exp // te, te, D).transpose(0, 2, 1).astype(BF16)
    out = _experts(h2, e1, e2, thr, peer_u[0].astype(BF16), vt_blocks,
                   x1, mod3, final_g.reshape(1, D), seq=S, tm=tm, te=te)
    return out.reshape(B, S, D)
```

```python
import functools
import math

import jax
import jax.numpy as jnp
from jax import lax
from jax.experimental import pallas as pl
from jax.experimental.pallas import tpu as pltpu

F32 = jnp.float32
BF16 = jnp.bfloat16

HEAD_DIM = 128
ROPE_DIM = HEAD_DIM // 4
ROPE_HALF = ROPE_DIM // 2
ROPE_THETA = 500000.0
POOL_SIZES = (2, 4, 8, 16)
DILATIONS = (1, 4, 16)
ATTN_BLOCK = 128
TOPK = 16
NORM_EPS = 1e-6
NEG_BIG = -1e30

LANES = 128
SUBLANES = 8
VMEM_LIMIT = 56 * 1024 * 1024

ATTN_SPAN = ATTN_BLOCK * max(DILATIONS)
POOL_HALO = 16


def _params(sem, vmem=VMEM_LIMIT):
    return pltpu.CompilerParams(dimension_semantics=sem, vmem_limit_bytes=vmem)


def _mod_kernel(c_ref, w_ref, b_ref, o_ref):
    c = c_ref[...]
    a = c * jax.nn.sigmoid(c)
    o_ref[...] = jnp.dot(a, w_ref[...], preferred_element_type=F32,
                         precision=lax.Precision.HIGHEST) + b_ref[...]


def _modulation(c, w_mod, b_mod):
    B, D = c.shape
    N = w_mod.shape[1]
    tn = 2048
    return pl.pallas_call(
        _mod_kernel,
        grid=(N // tn,),
        in_specs=[pl.BlockSpec((B, D), lambda j: (0, 0)),
                  pl.BlockSpec((D, tn), lambda j: (0, j)),
                  pl.BlockSpec((1, tn), lambda j: (0, j))],
        out_specs=pl.BlockSpec((B, tn), lambda j: (0, j)),
        out_shape=jax.ShapeDtypeStruct((B, N), F32),
        compiler_params=_params(("arbitrary",)),
        name="modulation",
    )(c, w_mod, b_mod.reshape(1, N))


def _rmsnorm_mod(x, g, shift, scale):
    ms = jnp.mean(x * x, axis=-1, keepdims=True)
    y = x * lax.rsqrt(ms + NORM_EPS) * g
    return y * (1.0 + scale) + shift


def _rope_kernel(pos_ref, inv_ref, cos_ref, sin_ref):
    ang = pos_ref[...].astype(F32) * inv_ref[...]
    cos_ref[...] = jnp.cos(ang)
    sin_ref[...] = jnp.sin(ang)


def _rope_tables(positions):
    T = positions.size
    per_row = LANES // ROPE_HALF
    pos_rep = jnp.repeat(positions.reshape(T // per_row, per_row), ROPE_HALF, axis=1)
    inv = ROPE_THETA ** (-jnp.arange(ROPE_HALF, dtype=F32) * 2.0 / ROPE_DIM)
    inv_rep = jnp.tile(inv, per_row).reshape(1, LANES)
    dense = pl.BlockSpec((T // per_row, LANES), lambda: (0, 0))
    cos_c, sin_c = pl.pallas_call(
        _rope_kernel,
        in_specs=[dense, pl.BlockSpec((1, LANES), lambda: (0, 0))],
        out_specs=[dense, dense],
        out_shape=[jax.ShapeDtypeStruct((T // per_row, LANES), F32)] * 2,
        name="rope_tables",
    )(pos_rep, inv_rep)
    cos_h, sin_h = cos_c.reshape(T, ROPE_HALF), sin_c.reshape(T, ROPE_HALF)
    zero_h = jnp.zeros_like(sin_h)
    rest = LANES - ROPE_DIM
    cos = jnp.concatenate([cos_h, cos_h, jnp.ones((T, rest), F32)], axis=1)
    sin_lo = jnp.concatenate([sin_h, zero_h, jnp.zeros((T, rest), F32)], axis=1)
    sin_hi = jnp.concatenate([zero_h, sin_h, jnp.zeros((T, rest), F32)], axis=1)
    return cos, sin_lo, sin_hi


def _inproj_kernel(x_ref, mod_ref, g_ref, cos_ref, sinlo_ref, sinhi_ref, w_ref, wpool_ref, pscale_ref,
                   qkv_ref, pool_ref, carry_scr, ext_scr,
                   *, tm, tiles_per_seq, attn_width):
    i = pl.program_id(0)
    aw = attn_width
    n_heads = aw // HEAD_DIM
    h = _rmsnorm_mod(x_ref[...], g_ref[...], mod_ref[0, 0:1, :], mod_ref[0, 1:2, :]).astype(BF16)

    def project(j):
        return jnp.dot(h, w_ref[:, j * aw:(j + 1) * aw], preferred_element_type=F32)

    def rotary(z, col0, scale):
        cos = cos_ref[...]
        sin_lo = sinlo_ref[...]
        sin_hi = sinhi_ref[...]
        for hh in range(n_heads):
            zh = z[:, hh * HEAD_DIM:(hh + 1) * HEAD_DIM]
            up = pltpu.roll(zh, HEAD_DIM - ROPE_HALF, axis=1)
            dn = pltpu.roll(zh, ROPE_HALF, axis=1)
            out = zh * cos - up * sin_lo + dn * sin_hi
            if scale is not None:
                out = out * scale
            qkv_ref[:, col0 + hh * HEAD_DIM:col0 + (hh + 1) * HEAD_DIM] = out

    rotary(project(0), 0, HEAD_DIM ** -0.5)
    rotary(project(1), aw, None)
    qkv_ref[:, 2 * aw:3 * aw] = project(2)

    u = project(3)
    first = (i % tiles_per_seq) == 0

    @pl.when(first)
    def _():
        carry_scr[...] = jnp.zeros_like(carry_scr)

    ext_scr[0:POOL_HALO, :] = carry_scr[...]
    ext_scr[POOL_HALO:POOL_HALO + tm, :] = u
    carry_scr[...] = u[tm - POOL_HALO:tm, :]
    gw = aw // len(POOL_SIZES)
    t_in_seq = (i % tiles_per_seq) * tm + lax.broadcasted_iota(jnp.int32, (tm, gw), 0)
    for gi, p in enumerate(POOL_SIZES):
        cols = slice(gi * gw, (gi + 1) * gw)
        u_g = u[:, cols]
        acc = u_g
        for back in range(1, p):
            acc = acc + ext_scr[POOL_HALO - back:POOL_HALO - back + tm, cols]
        cnt = jnp.minimum(t_in_seq + 1, p).astype(F32)
        r = acc / cnt - u_g
        y = jnp.dot(r.astype(BF16), wpool_ref[gi], preferred_element_type=F32)
        pool_ref[:, cols] = (y * pscale_ref[:, cols]).astype(pool_ref.dtype)


def _in_projection(x2d, mod3, norm_g, rope, w_in, w_pool, pool_scale, *, seq, tm):
    T, D = x2d.shape
    n_in = w_in.shape[1]
    aw = n_in // 4
    tiles_per_seq = seq // tm
    kern = functools.partial(_inproj_kernel, tm=tm, tiles_per_seq=tiles_per_seq, attn_width=aw)
    return pl.pallas_call(
        kern,
        grid=(T // tm,),
        in_specs=[
            pl.BlockSpec((tm, D), lambda i: (i, 0)),
            pl.BlockSpec((1, 6, D), lambda i: (i // tiles_per_seq, 0, 0)),
            pl.BlockSpec((1, D), lambda i: (0, 0)),
            pl.BlockSpec((tm, LANES), lambda i: (i, 0)),
            pl.BlockSpec((tm, LANES), lambda i: (i, 0)),
            pl.BlockSpec((tm, LANES), lambda i: (i, 0)),
            pl.BlockSpec((D, n_in), lambda i: (0, 0), pipeline_mode=pl.Buffered(1)),
            pl.BlockSpec(w_pool.shape, lambda i: (0, 0, 0)),
            pl.BlockSpec((1, aw), lambda i: (0, 0)),
        ],
        out_specs=[
            pl.BlockSpec((tm, 3 * aw), lambda i: (i, 0)),
            pl.BlockSpec((tm, aw), lambda i: (i, 0)),
        ],
        out_shape=[jax.ShapeDtypeStruct((T, 3 * aw), F32),
                   jax.ShapeDtypeStruct((T, aw), BF16)],
        scratch_shapes=[
            pltpu.VMEM((POOL_HALO, aw), F32),
            pltpu.VMEM((POOL_HALO + tm, aw), F32),
        ],
        compiler_params=_params(("arbitrary",)),
        name="in_projection",
    )(x2d, mod3, norm_g, *rope, w_in, w_pool, pool_scale)


def _strided(start, size, stride):
    return pl.ds(start, size) if stride == 1 else pl.ds(start, size, stride=stride)


def _attn_kernel(q_ref, kp_ref, kc_ref, vp_ref, vc_ref, o_ref, o_scr, l_scr, plane_scr, res_scr):
    n = pl.program_id(2)
    blk = ATTN_BLOCK
    qi = lax.broadcasted_iota(jnp.int32, (blk, 2 * blk), 0)
    kj = lax.broadcasted_iota(jnp.int32, (blk, 2 * blk), 1)
    dist = qi + blk - kj
    band = (dist >= 0) & (dist <= blk)
    band_first = band & ((kj >= blk) | (n > 0))

    def block(q, k_lo, k_hi, v_lo, v_hi, mask):
        k = jnp.concatenate([k_lo.astype(BF16), k_hi.astype(BF16)], axis=0)
        v = jnp.concatenate([v_lo.astype(BF16), v_hi.astype(BF16)], axis=0)
        s = lax.dot_general(q.astype(BF16), k, (((1,), (1,)), ((), ())),
                            preferred_element_type=F32)
        s = jnp.where(mask, s, NEG_BIG)
        m = jnp.max(s, axis=-1, keepdims=True)
        p = jnp.exp(s - m)
        den = jnp.sum(p, axis=-1, keepdims=True)
        o = jnp.dot(p.astype(BF16), v, preferred_element_type=F32) / den
        return o, jnp.broadcast_to(m + jnp.log(den), (blk, HEAD_DIM))

    for pi, d in enumerate(DILATIONS[:2]):
        for r in range(d):
            k_hi = v_hi = None
            for c in range(ATTN_SPAN // (blk * d)):
                start = r + d * blk * c
                rows = _strided(start, blk, d)
                if c == 0:
                    lo_rows = _strided(ATTN_SPAN + r - d * blk, blk, d)
                    k_lo, v_lo = kp_ref[lo_rows, :], vp_ref[lo_rows, :]
                else:
                    k_lo, v_lo = k_hi, v_hi
                k_hi, v_hi = kc_ref[rows, :].astype(BF16), vc_ref[rows, :].astype(BF16)
                o, l = block(q_ref[rows, :], k_lo, k_hi, v_lo, v_hi,
                             band_first if c == 0 else band)
                o_scr[pi, rows, :] = o
                l_scr[pi, rows, :] = l

    sub, per_plane = 4, ATTN_SPAN // 4
    assert DILATIONS[2] == sub * sub and ATTN_SPAN == blk * DILATIONS[2]
    for rho in range(sub):
        plane = pl.ds(rho, per_plane, stride=sub)
        for t, src in enumerate((q_ref, kp_ref, kc_ref, vp_ref, vc_ref)):
            plane_scr[t] = src[plane, :]
        for c2 in range(sub):
            rows = pl.ds(c2, blk, stride=sub)
            o, l = block(plane_scr[0, rows, :], plane_scr[1, rows, :], plane_scr[2, rows, :],
                         plane_scr[3, rows, :], plane_scr[4, rows, :], band_first)
            res_scr[0, rows, :] = o
            res_scr[1, rows, :] = l
        o_scr[2, plane, :] = res_scr[0]
        l_scr[2, plane, :] = res_scr[1]

    l0, l1, l2 = l_scr[0], l_scr[1], l_scr[2]
    lmax = jnp.maximum(jnp.maximum(l0, l1), l2)
    e0, e1, e2 = jnp.exp(l0 - lmax), jnp.exp(l1 - lmax), jnp.exp(l2 - lmax)
    mixed = (e0 * o_scr[0] + e1 * o_scr[1] + e2 * o_scr[2]) / (e0 + e1 + e2)
    o_ref[...] = mixed.astype(o_ref.dtype)


def _attention(qkv, *, batch, seq, attn_width):
    T = qkv.shape[0]
    n_heads = attn_width // HEAD_DIM
    spans = seq // ATTN_SPAN
    blk = (ATTN_SPAN, HEAD_DIM)

    def cur(col0):
        return lambda b, h, n: (b * spans + n, col0 + h)

    def prev(col0):
        return lambda b, h, n: (b * spans + jnp.maximum(n - 1, 0), col0 + h)

    return pl.pallas_call(
        _attn_kernel,
        grid=(batch, n_heads, spans),
        in_specs=[pl.BlockSpec(blk, cur(0)),
                  pl.BlockSpec(blk, prev(n_heads)),
                  pl.BlockSpec(blk, cur(n_heads)),
                  pl.BlockSpec(blk, prev(2 * n_heads)),
                  pl.BlockSpec(blk, cur(2 * n_heads))],
        out_specs=pl.BlockSpec(blk, cur(0)),
        out_shape=jax.ShapeDtypeStruct((T, attn_width), BF16),
        scratch_shapes=[pltpu.VMEM((len(DILATIONS),) + blk, F32),
                        pltpu.VMEM((len(DILATIONS),) + blk, F32),
                        pltpu.VMEM((5, ATTN_SPAN // 4, HEAD_DIM), F32),
                        pltpu.VMEM((2, ATTN_SPAN // 4, HEAD_DIM), F32)],
        compiler_params=_params(("arbitrary", "arbitrary", "arbitrary")),
        name="dilated_attention",
    )(qkv, qkv, qkv, qkv, qkv)


def _outproj_kernel(attn_ref, pool_ref, x_ref, mod_ref, g_ref, w_ref, x1_ref, h2_ref, *, attn_width):
    mix = jnp.dot(attn_ref[...], w_ref[0:attn_width, :], preferred_element_type=F32)
    mix = mix + jnp.dot(pool_ref[...], w_ref[attn_width:, :], preferred_element_type=F32)
    x1 = x_ref[...] + mod_ref[0, 2:3, :] * mix
    x1_ref[...] = x1
    h2 = _rmsnorm_mod(x1, g_ref[...], mod_ref[0, 3:4, :], mod_ref[0, 4:5, :])
    h2_ref[...] = h2.astype(BF16)


def _out_projection(attn, pool, x2d, mod3, norm_g, w_out, *, seq, tm):
    T, D = x2d.shape
    aw = attn.shape[1]
    pw = pool.shape[1]
    tiles_per_seq = seq // tm
    return pl.pallas_call(
        functools.partial(_outproj_kernel, attn_width=aw),
        grid=(T // tm,),
        in_specs=[
            pl.BlockSpec((tm, aw), lambda i: (i, 0)),
            pl.BlockSpec((tm, pw), lambda i: (i, 0)),
            pl.BlockSpec((tm, D), lambda i: (i, 0)),
            pl.BlockSpec((1, 6, D), lambda i: (i // tiles_per_seq, 0, 0)),
            pl.BlockSpec((1, D), lambda i: (0, 0)),
            pl.BlockSpec((aw + pw, D), lambda i: (0, 0)),
        ],
        out_specs=[pl.BlockSpec((tm, D), lambda i: (i, 0)),
                   pl.BlockSpec((tm, D), lambda i: (i, 0))],
        out_shape=[jax.ShapeDtypeStruct((T, D), F32),
                   jax.ShapeDtypeStruct((T, D), BF16)],
        compiler_params=_params(("arbitrary",)),
        name="out_projection",
    )(attn, pool, x2d, mod3, norm_g, w_out)


def _oddeven_merge_sort_pairs(n):
    pairs = []

    def merge(lo, hi, r):
        step = r * 2
        if step < hi - lo:
            merge(lo, hi, step)
            merge(lo + r, hi, step)
            for k in range(lo + r, hi - r, step):
                pairs.append((k, k + r))
        else:
            pairs.append((lo, lo + r))

    def sort(lo, hi):
        if hi - lo >= 1:
            mid = lo + (hi - lo) // 2
            sort(lo, mid)
            sort(mid + 1, hi)
            merge(lo, hi, 1)

    sort(0, n - 1)
    return pairs


_SORT16 = _oddeven_merge_sort_pairs(TOPK)


def _sort_desc(vals):
    vals = list(vals)
    for a, b in _SORT16:
        hi, lo = jnp.maximum(vals[a], vals[b]), jnp.minimum(vals[a], vals[b])
        vals[a], vals[b] = hi, lo
    return vals


def _merge_top(a_list, b_list):
    n = TOPK
    c = [jnp.maximum(a_list[k], b_list[n - 1 - k]) for k in range(n)]
    stride = n // 2
    while stride >= 1:
        for k in range(n):
            if (k & stride) == 0:
                hi, lo = jnp.maximum(c[k], c[k + stride]), jnp.minimum(c[k], c[k + stride])
                c[k], c[k + stride] = hi, lo
        stride //= 2
    return c


_CAND_ROWS = [[(i, j) for j in range(TOPK) if (i + 1) * (j + 1) <= TOPK] for i in range(TOPK)]


def _route_kernel(h_ref, wq_ref, keys_ref, e1_ref, e2_ref, thr_ref, qp_scr, top_scr,
                  *, tm, n_heads, n_keys):
    qp = jnp.dot(h_ref[...], wq_ref[...], preferred_element_type=F32)
    qp_scr[...] = qp.astype(BF16)
    n_chunks = tm // LANES
    groups = n_keys // SUBLANES
    assert groups == TOPK

    for h in range(n_heads):
        for half, out_ref in ((0, e1_ref), (1, e2_ref)):
            col0 = (2 * h + half) * n_keys
            s_t = lax.dot_general(keys_ref[h, half], qp_scr[:, col0:col0 + n_keys],
                                  (((1,), (1,)), ((), ())), preferred_element_type=F32)
            out_ref[h] = s_t
            for cidx in range(n_chunks):
                lanes = slice(cidx * LANES, (cidx + 1) * LANES)
                blk = s_t[:, lanes]
                vals = _sort_desc([blk[g * SUBLANES:(g + 1) * SUBLANES, :] for g in range(groups)])
                for shift in (4, 2, 1):
                    partner = [pltpu.roll(v, shift, axis=0) for v in vals]
                    vals = _merge_top(vals, partner)
                for k in range(TOPK):
                    top_scr[half, k, h:h + 1, lanes] = vals[k][0:1, :]

    for cidx in range(n_chunks):
        lanes = slice(cidx * LANES, (cidx + 1) * LANES)
        a = [top_scr[0, k, :, lanes] for k in range(TOPK)]
        b = [top_scr[1, k, :, lanes] for k in range(TOPK)]
        ea = [jnp.exp(v - a[0]) for v in a]
        eb = [jnp.exp(v - b[0]) for v in b]
        pad = jnp.full_like(a[0], -1.0)
        best = None
        for row in _CAND_ROWS:
            lst = [ea[i] * eb[j] for (i, j) in row]
            lst = lst + [pad] * (TOPK - len(lst))
            best = lst if best is None else _merge_top(best, lst)
        z = best[0]
        for k in range(1, TOPK):
            z = z + best[k]
        rz = 1.0 / z
        cut = best[TOPK - 1]
        ebn = [v * rz for v in eb]
        thr = None
        for row in _CAND_ROWS:
            for (i, j) in row:
                sel = ea[i] * eb[j] >= cut
                cand = jnp.where(sel, ea[i] * ebn[j], jnp.inf)
                thr = cand if thr is None else jnp.minimum(thr, cand)
        thr_ref[:, lanes] = 0.5 * thr
        top_scr[0, 0, :, lanes] = a[0]
        top_scr[1, 0, :, lanes] = b[0]
        top_scr[1, 1, :, lanes] = rz

    for h in range(n_heads):
        m1 = top_scr[0, 0, h:h + 1, :]
        m2 = top_scr[1, 0, h:h + 1, :]
        rz = top_scr[1, 1, h:h + 1, :]
        e1_ref[h] = 0.5 * jnp.exp(e1_ref[h] - m1)
        e2_ref[h] = jnp.exp(e2_ref[h] - m2) * rz


def _routing(h2, w_query, sub_keys, *, tm):
    T, D = h2.shape
    n_heads, _, n_keys, kd = sub_keys.shape
    qw = w_query.shape[1]
    kern = functools.partial(_route_kernel, tm=tm, n_heads=n_heads, n_keys=n_keys)
    key_blk = pl.BlockSpec((n_heads, n_keys, tm), lambda i: (0, 0, i))
    return pl.pallas_call(
        kern,
        grid=(T // tm,),
        in_specs=[pl.BlockSpec((tm, D), lambda i: (i, 0)),
                  pl.BlockSpec((D, qw), lambda i: (0, 0)),
                  pl.BlockSpec(sub_keys.shape, lambda i: (0, 0, 0, 0))],
        out_specs=[key_blk, key_blk, pl.BlockSpec((n_heads, tm), lambda i: (0, i))],
        out_shape=[jax.ShapeDtypeStruct((n_heads, n_keys, T), F32),
                   jax.ShapeDtypeStruct((n_heads, n_keys, T), F32),
                   jax.ShapeDtypeStruct((n_heads, T), F32)],
        scratch_shapes=[pltpu.VMEM((tm, qw), BF16),
                        pltpu.VMEM((2, TOPK, n_heads, tm), F32)],
        compiler_params=_params(("arbitrary",)),
        name="peer_routing",
    )(h2, w_query, sub_keys)


def _twice_gelu(a):
    return a * (lax.erf(a * (1.0 / math.sqrt(2.0))) + 1.0)


def _expert_kernel(h_ref, e1_ref, e2_ref, thr_ref, u_ref, vt_ref, x1_ref, mod_ref, g_ref,
                   o_ref, acc_scr, act_scr, *, tm, te, n_heads, n_keys):
    j = pl.program_id(1)
    n_i1 = te // n_keys
    n_chunks = tm // LANES

    @pl.when(j == 0)
    def _():
        acc_scr[...] = jnp.zeros_like(acc_scr)

    a_t = lax.dot_general(u_ref[...], h_ref[...], (((1,), (1,)), ((), ())),
                          preferred_element_type=F32)
    for k in range(n_i1):
        for cidx in range(n_chunks):
            lanes = slice(cidx * LANES, (cidx + 1) * LANES)
            gate = None
            for h in range(n_heads):
                p = e1_ref[h, k:k + 1, lanes] * e2_ref[h, :, lanes]
                sel = jnp.where(p >= thr_ref[h:h + 1, lanes], p, 0.0)
                gate = sel if gate is None else gate + sel
            a_blk = a_t[k * n_keys:(k + 1) * n_keys, lanes]
            act_scr[k * n_keys:(k + 1) * n_keys, lanes] = (_twice_gelu(a_blk) * gate).astype(BF16)

    for c in range(2):
        cols = slice(c * tm // 2, (c + 1) * tm // 2)
        acc_scr[:, cols] += jnp.dot(vt_ref[...], act_scr[:, cols], preferred_element_type=F32)

    @pl.when(j == pl.num_programs(1) - 1)
    def _():
        y = acc_scr[...].T
        x2 = x1_ref[...] + mod_ref[0, 5:6, :] * y
        ms = jnp.mean(x2 * x2, axis=-1, keepdims=True)
        o_ref[...] = x2 * lax.rsqrt(ms + NORM_EPS) * g_ref[...]


def _experts(h2, e1, e2, thr, peer_u, peer_vt, x1, mod3, final_g, *, seq, tm, te):
    T, D = h2.shape
    n_heads, n_keys, _ = e1.shape
    E = peer_u.shape[0]
    tiles_per_seq = seq // tm
    n_i1 = te // n_keys
    assert n_i1 == SUBLANES, "expert rows of one block fill the sublanes of an f32 tile"
    kern = functools.partial(_expert_kernel, tm=tm, te=te, n_heads=n_heads, n_keys=n_keys)
    row_blk = pl.BlockSpec((n_heads, n_i1, tm), lambda i, j: (0, j, i))
    key_blk = pl.BlockSpec((n_heads, n_keys, tm), lambda i, j: (0, 0, i))
    return pl.pallas_call(
        kern,
        grid=(T // tm, E // te),
        in_specs=[
            pl.BlockSpec((tm, D), lambda i, j: (i, 0)),
            row_blk, key_blk,
            pl.BlockSpec((n_heads, tm), lambda i, j: (0, i)),
            pl.BlockSpec((te, D), lambda i, j: (j, 0)),
            pl.BlockSpec((None, D, te), lambda i, j: (j, 0, 0)),
            pl.BlockSpec((tm, D), lambda i, j: (i, 0)),
            pl.BlockSpec((1, 6, D), lambda i, j: (i // tiles_per_seq, 0, 0)),
            pl.BlockSpec((1, D), lambda i, j: (0, 0)),
        ],
        out_specs=pl.BlockSpec((tm, D), lambda i, j: (i, 0)),
        out_shape=jax.ShapeDtypeStruct((T, D), F32),
        scratch_shapes=[pltpu.VMEM((D, tm), F32),
                        pltpu.VMEM((te, tm), BF16)],
        compiler_params=_params(("arbitrary", "arbitrary")),
        name="peer_experts",
    )(h2, e1, e2, thr, peer_u, peer_vt, x1, mod3, final_g)


def kernel(x, c, positions, w_mod, b_mod, norm1_g, w_in, w_pool, pool_scale, w_out, norm2_g,
           w_query, sub_keys, peer_u, peer_v, final_g):
    B, S, D = x.shape
    assert w_mod.shape[0] == 1, "single-layer stack"
    assert S % ATTN_SPAN == 0
    tm = min(512, S)
    aw = w_in.shape[2] // 4
    x2d = x.reshape(B * S, D)

    mod3 = _modulation(c, w_mod[0], b_mod[0]).reshape(B, 6, D)
    qkv, pool = _in_projection(x2d, mod3, norm1_g[0].reshape(1, D), _rope_tables(positions),
                               w_in[0].astype(BF16), w_pool[0].astype(BF16),
                               pool_scale[0].reshape(1, -1), seq=S, tm=tm)
    attn = _attention(qkv, batch=B, seq=S, attn_width=aw)
    x1, h2 = _out_projection(attn, pool, x2d, mod3, norm2_g[0].reshape(1, D),
                             w_out[0].astype(BF16), seq=S, tm=tm)
    e1, e2, thr = _routing(h2, w_query[0].astype(BF16), sub_keys[0].astype(BF16), tm=tm)
    te = 1024
    n_exp = peer_v.shape[1]
    vt_blocks = peer_v[0].reshape(n_exp // te, te, D).transpose(0, 2, 1).astype(BF16)
    out = _experts(h2, e1, e2, thr, peer_u[0].astype(BF16), vt_blocks,
                   x1, mod3, final_g.reshape(1, D), seq=S, tm=tm, te=te)
    return out.reshape(B, S, D)
```

```python
import functools
import math

import jax
import jax.numpy as jnp
from jax import lax
from jax.experimental import pallas as pl
from jax.experimental.pallas import tpu as pltpu

F32 = jnp.float32
BF16 = jnp.bfloat16

HEAD_DIM = 128
ROPE_DIM = HEAD_DIM // 4
ROPE_HALF = ROPE_DIM // 2
ROPE_THETA = 500000.0
POOL_SIZES = (2, 4, 8, 16)
DILATIONS = (1, 4, 16)
ATTN_BLOCK = 128
TOPK = 16
NORM_EPS = 1e-6
NEG_BIG = -1e30
PRE_SCALE = 1.0 / math.sqrt(2.0)
GATE_SCALE = 1.0 / math.sqrt(2.0)

LANES = 128
SUBLANES = 8
VMEM_LIMIT = 56 * 1024 * 1024

ATTN_SPAN = ATTN_BLOCK * max(DILATIONS)
POOL_HALO = 16


def _params(sem, vmem=VMEM_LIMIT):
    return pltpu.CompilerParams(dimension_semantics=sem, vmem_limit_bytes=vmem)


def _mod_kernel(c_ref, w_ref, b_ref, o_ref, *, unroll):
    B, D, _ = c_ref.shape
    tn = w_ref.shape[1]
    n_chunks = tn // LANES

    def group(g, acc):
        rows = pl.ds(pl.multiple_of(g * SUBLANES, SUBLANES), SUBLANES)
        cc = c_ref[:, rows, :]
        a = cc * jax.nn.sigmoid(cc)
        out = []
        for n in range(n_chunks):
            w = w_ref[rows, n * LANES:(n + 1) * LANES]
            out.extend(acc[n * B + b] + a[b] * w for b in range(B))
        return tuple(out)

    zero = jnp.zeros((SUBLANES, LANES), F32)
    acc = lax.fori_loop(0, D // SUBLANES, group, (zero,) * (B * n_chunks), unroll=unroll)
    for n in range(n_chunks):
        lanes = slice(n * LANES, (n + 1) * LANES)
        for b in range(B):
            o_ref[b:b + 1, lanes] = (jnp.sum(acc[n * B + b], axis=0, keepdims=True)
                                     + b_ref[:, lanes])


def _modulation(c, w_mod, b_mod):
    B, D = c.shape
    N = w_mod.shape[1]
    tn = 1024
    c_rep = jnp.broadcast_to(c[:, :, None], (B, D, LANES))
    return pl.pallas_call(
        functools.partial(_mod_kernel, unroll=4),
        grid=(N // tn,),
        in_specs=[pl.BlockSpec((B, D, LANES), lambda j: (0, 0, 0)),
                  pl.BlockSpec((D, tn), lambda j: (0, j)),
                  pl.BlockSpec((1, tn), lambda j: (0, j))],
        out_specs=pl.BlockSpec((B, tn), lambda j: (0, j)),
        out_shape=jax.ShapeDtypeStruct((B, N), F32),
        compiler_params=_params(("arbitrary",)),
        name="modulation",
    )(c_rep, w_mod, b_mod.reshape(1, N))


def _rmsnorm_mod(x, g, shift, scale):
    ms = jnp.mean(x * x, axis=-1, keepdims=True)
    y = x * lax.rsqrt(ms + NORM_EPS) * g
    return y * (1.0 + scale) + shift


def _rope_kernel(pos_ref, inv_ref, cos_ref, sinlo_ref, sinhi_ref, cos_scr, sin_scr, *, n_blocks):
    step = pl.program_id(0)

    @pl.when(step == 0)
    def _():
        ang = pos_ref[...].astype(F32) * inv_ref[...]
        cos_scr[...] = jnp.cos(ang)
        sin_scr[...] = jnp.sin(ang)

    lane = lax.broadcasted_iota(jnp.int32, cos_scr.shape, 1)
    first = lane < ROPE_HALF
    second = (lane >= ROPE_HALF) & (lane < ROPE_DIM)
    for k in range(n_blocks):
        @pl.when(step == k)
        def _(k=k):
            c, s = cos_scr[...], sin_scr[...]
            if k:
                c = pltpu.roll(c, LANES - k * ROPE_HALF, axis=1)
                s = pltpu.roll(s, LANES - k * ROPE_HALF, axis=1)
            c_up = pltpu.roll(c, ROPE_HALF, axis=1)
            s_up = pltpu.roll(s, ROPE_HALF, axis=1)
            cos_ref[...] = jnp.where(first, c, jnp.where(second, c_up, 1.0))
            sinlo_ref[...] = jnp.where(first, s, 0.0)
            sinhi_ref[...] = jnp.where(second, s_up, 0.0)


def _rope_tables(positions):
    T = positions.size
    n_blocks = LANES // ROPE_HALF
    rows = T // n_blocks
    pos_rep = jnp.repeat(positions.reshape(n_blocks, rows).T, ROPE_HALF, axis=1)
    inv = ROPE_THETA ** (-jnp.arange(ROPE_HALF, dtype=F32) * 2.0 / ROPE_DIM)
    inv_rep = jnp.tile(inv, n_blocks).reshape(1, LANES)
    dense = pl.BlockSpec((rows, LANES), lambda k: (0, 0))
    spread = pl.BlockSpec((rows, LANES), lambda k: (k, 0))
    return pl.pallas_call(
        functools.partial(_rope_kernel, n_blocks=n_blocks),
        grid=(n_blocks,),
        in_specs=[dense, pl.BlockSpec((1, LANES), lambda k: (0, 0))],
        out_specs=[spread, spread, spread],
        out_shape=[jax.ShapeDtypeStruct((T, LANES), F32)] * 3,
        scratch_shapes=[pltpu.VMEM((rows, LANES), F32), pltpu.VMEM((rows, LANES), F32)],
        compiler_params=_params(("arbitrary",)),
        name="rope_tables",
    )(pos_rep, inv_rep)


def _inproj_kernel(x_ref, mod_ref, g_ref, cos_ref, sinlo_ref, sinhi_ref, w_ref, wpool_ref, pscale_ref,
                   qkv_ref, pool_ref, carry_scr, ext_scr,
                   *, tm, tiles_per_seq, attn_width):
    i = pl.program_id(0)
    aw = attn_width
    n_heads = aw // HEAD_DIM
    h = _rmsnorm_mod(x_ref[...], g_ref[...], mod_ref[0, 0:1, :], mod_ref[0, 1:2, :]).astype(BF16)

    def project(j):
        return jnp.dot(h, w_ref[:, j * aw:(j + 1) * aw], preferred_element_type=F32)

    def rotary(z, col0, scale):
        cos = cos_ref[...]
        sin_lo = sinlo_ref[...]
        sin_hi = sinhi_ref[...]
        for hh in range(n_heads):
            zh = z[:, hh * HEAD_DIM:(hh + 1) * HEAD_DIM]
            up = pltpu.roll(zh, HEAD_DIM - ROPE_HALF, axis=1)
            dn = pltpu.roll(zh, ROPE_HALF, axis=1)
            out = zh * cos - up * sin_lo + dn * sin_hi
            if scale is not None:
                out = out * scale
            qkv_ref[:, col0 + hh * HEAD_DIM:col0 + (hh + 1) * HEAD_DIM] = out

    rotary(project(0), 0, HEAD_DIM ** -0.5)
    rotary(project(1), aw, None)
    qkv_ref[:, 2 * aw:3 * aw] = project(2)

    u = project(3)
    first = (i % tiles_per_seq) == 0

    @pl.when(first)
    def _():
        carry_scr[...] = jnp.zeros_like(carry_scr)

    ext_scr[0:POOL_HALO, :] = carry_scr[...]
    ext_scr[POOL_HALO:POOL_HALO + tm, :] = u
    carry_scr[...] = u[tm - POOL_HALO:tm, :]
    gw = aw // len(POOL_SIZES)
    t_in_seq = (i % tiles_per_seq) * tm + lax.broadcasted_iota(jnp.int32, (tm, gw), 0)
    for gi, p in enumerate(POOL_SIZES):
        cols = slice(gi * gw, (gi + 1) * gw)
        u_g = u[:, cols]
        acc = u_g
        for back in range(1, p):
            acc = acc + ext_scr[POOL_HALO - back:POOL_HALO - back + tm, cols]
        cnt = jnp.minimum(t_in_seq + 1, p).astype(F32)
        r = acc / cnt - u_g
        y = jnp.dot(r.astype(BF16), wpool_ref[gi], preferred_element_type=F32)
        pool_ref[:, cols] = (y * pscale_ref[:, cols]).astype(pool_ref.dtype)


def _in_projection(x2d, mod3, norm_g, rope, w_in, w_pool, pool_scale, *, seq, tm):
    T, D = x2d.shape
    n_in = w_in.shape[1]
    aw = n_in // 4
    tiles_per_seq = seq // tm
    kern = functools.partial(_inproj_kernel, tm=tm, tiles_per_seq=tiles_per_seq, attn_width=aw)
    return pl.pallas_call(
        kern,
        grid=(T // tm,),
        in_specs=[
            pl.BlockSpec((tm, D), lambda i: (i, 0)),
            pl.BlockSpec((1, 6, D), lambda i: (i // tiles_per_seq, 0, 0)),
            pl.BlockSpec((1, D), lambda i: (0, 0)),
            pl.BlockSpec((tm, LANES), lambda i: (i, 0)),
            pl.BlockSpec((tm, LANES), lambda i: (i, 0)),
            pl.BlockSpec((tm, LANES), lambda i: (i, 0)),
            pl.BlockSpec((D, n_in), lambda i: (0, 0), pipeline_mode=pl.Buffered(1)),
            pl.BlockSpec(w_pool.shape, lambda i: (0, 0, 0)),
            pl.BlockSpec((1, aw), lambda i: (0, 0)),
        ],
        out_specs=[
            pl.BlockSpec((tm, 3 * aw), lambda i: (i, 0)),
            pl.BlockSpec((tm, aw), lambda i: (i, 0)),
        ],
        out_shape=[jax.ShapeDtypeStruct((T, 3 * aw), F32),
                   jax.ShapeDtypeStruct((T, aw), BF16)],
        scratch_shapes=[
            pltpu.VMEM((POOL_HALO, aw), F32),
            pltpu.VMEM((POOL_HALO + tm, aw), F32),
        ],
        compiler_params=_params(("arbitrary",)),
        name="in_projection",
    )(x2d, mod3, norm_g, *rope, w_in, w_pool, pool_scale)


def _strided(start, size, stride):
    return pl.ds(start, size) if stride == 1 else pl.ds(start, size, stride=stride)


def _attn_kernel(q_ref, kp_ref, kc_ref, vp_ref, vc_ref, o_ref, o_scr, l_scr, plane_scr, res_scr):
    n = pl.program_id(2)
    blk = ATTN_BLOCK
    qi = lax.broadcasted_iota(jnp.int32, (blk, 2 * blk), 0)
    kj = lax.broadcasted_iota(jnp.int32, (blk, 2 * blk), 1)
    dist = qi + blk - kj
    band = (dist >= 0) & (dist <= blk)
    band_first = band & ((kj >= blk) | (n > 0))

    def block(q, k_lo, k_hi, v_lo, v_hi, mask):
        k = jnp.concatenate([k_lo.astype(BF16), k_hi.astype(BF16)], axis=0)
        v = jnp.concatenate([v_lo.astype(BF16), v_hi.astype(BF16)], axis=0)
        s = lax.dot_general(q.astype(BF16), k, (((1,), (1,)), ((), ())),
                            preferred_element_type=F32)
        s = jnp.where(mask, s, NEG_BIG)
        m = jnp.max(s, axis=-1, keepdims=True)
        p = jnp.exp(s - m)
        den = jnp.sum(p, axis=-1, keepdims=True)
        o = jnp.dot(p.astype(BF16), v, preferred_element_type=F32) / den
        return o, jnp.broadcast_to(m + jnp.log(den), (blk, HEAD_DIM))

    for pi, d in enumerate(DILATIONS[:2]):
        for r in range(d):
            k_hi = v_hi = None
            for c in range(ATTN_SPAN // (blk * d)):
                start = r + d * blk * c
                rows = _strided(start, blk, d)
                if c == 0:
                    lo_rows = _strided(ATTN_SPAN + r - d * blk, blk, d)
                    k_lo, v_lo = kp_ref[lo_rows, :], vp_ref[lo_rows, :]
                else:
                    k_lo, v_lo = k_hi, v_hi
                k_hi, v_hi = kc_ref[rows, :].astype(BF16), vc_ref[rows, :].astype(BF16)
                o, l = block(q_ref[rows, :], k_lo, k_hi, v_lo, v_hi,
                             band_first if c == 0 else band)
                o_scr[pi, rows, :] = o
                l_scr[pi, rows, :] = l

    sub, per_plane = 4, ATTN_SPAN // 4
    assert DILATIONS[2] == sub * sub and ATTN_SPAN == blk * DILATIONS[2]
    for rho in range(sub):
        plane = pl.ds(rho, per_plane, stride=sub)
        for t, src in enumerate((q_ref, kp_ref, kc_ref, vp_ref, vc_ref)):
            plane_scr[t] = src[plane, :]
        for c2 in range(sub):
            rows = pl.ds(c2, blk, stride=sub)
            o, l = block(plane_scr[0, rows, :], plane_scr[1, rows, :], plane_scr[2, rows, :],
                         plane_scr[3, rows, :], plane_scr[4, rows, :], band_first)
            res_scr[0, rows, :] = o
            res_scr[1, rows, :] = l
        o_scr[2, plane, :] = res_scr[0]
        l_scr[2, plane, :] = res_scr[1]

    l0, l1, l2 = l_scr[0], l_scr[1], l_scr[2]
    lmax = jnp.maximum(jnp.maximum(l0, l1), l2)
    e0, e1, e2 = jnp.exp(l0 - lmax), jnp.exp(l1 - lmax), jnp.exp(l2 - lmax)
    mixed = (e0 * o_scr[0] + e1 * o_scr[1] + e2 * o_scr[2]) / (e0 + e1 + e2)
    o_ref[...] = mixed.astype(o_ref.dtype)


def _attention(qkv, *, batch, seq, attn_width):
    T = qkv.shape[0]
    n_heads = attn_width // HEAD_DIM
    spans = seq // ATTN_SPAN
    blk = (ATTN_SPAN, HEAD_DIM)

    def cur(col0):
        return lambda b, h, n: (b * spans + n, col0 + h)

    def prev(col0):
        return lambda b, h, n: (b * spans + jnp.maximum(n - 1, 0), col0 + h)

    return pl.pallas_call(
        _attn_kernel,
        grid=(batch, n_heads, spans),
        in_specs=[pl.BlockSpec(blk, cur(0)),
                  pl.BlockSpec(blk, prev(n_heads)),
                  pl.BlockSpec(blk, cur(n_heads)),
                  pl.BlockSpec(blk, prev(2 * n_heads)),
                  pl.BlockSpec(blk, cur(2 * n_heads))],
        out_specs=pl.BlockSpec(blk, cur(0)),
        out_shape=jax.ShapeDtypeStruct((T, attn_width), BF16),
        scratch_shapes=[pltpu.VMEM((len(DILATIONS),) + blk, F32),
                        pltpu.VMEM((len(DILATIONS),) + blk, F32),
                        pltpu.VMEM((5, ATTN_SPAN // 4, HEAD_DIM), F32),
                        pltpu.VMEM((2, ATTN_SPAN // 4, HEAD_DIM), F32)],
        compiler_params=_params(("arbitrary", "arbitrary", "arbitrary")),
        name="dilated_attention",
    )(qkv, qkv, qkv, qkv, qkv)


def _outproj_kernel(attn_ref, pool_ref, x_ref, mod_ref, g_ref, w_ref, x1_ref, h2_ref, *, attn_width):
    mix = jnp.dot(attn_ref[...], w_ref[0:attn_width, :], preferred_element_type=F32)
    mix = mix + jnp.dot(pool_ref[...], w_ref[attn_width:, :], preferred_element_type=F32)
    x1 = x_ref[...] + mod_ref[0, 2:3, :] * mix
    x1_ref[...] = x1
    h2 = _rmsnorm_mod(x1, g_ref[...], mod_ref[0, 3:4, :], mod_ref[0, 4:5, :])
    h2_ref[...] = h2.astype(BF16)


def _out_projection(attn, pool, x2d, mod3, norm_g, w_out, *, seq, tm):
    T, D = x2d.shape
    aw = attn.shape[1]
    pw = pool.shape[1]
    tiles_per_seq = seq // tm
    return pl.pallas_call(
        functools.partial(_outproj_kernel, attn_width=aw),
        grid=(T // tm,),
        in_specs=[
            pl.BlockSpec((tm, aw), lambda i: (i, 0)),
            pl.BlockSpec((tm, pw), lambda i: (i, 0)),
            pl.BlockSpec((tm, D), lambda i: (i, 0)),
            pl.BlockSpec((1, 6, D), lambda i: (i // tiles_per_seq, 0, 0)),
            pl.BlockSpec((1, D), lambda i: (0, 0)),
            pl.BlockSpec((aw + pw, D), lambda i: (0, 0)),
        ],
        out_specs=[pl.BlockSpec((tm, D), lambda i: (i, 0)),
                   pl.BlockSpec((tm, D), lambda i: (i, 0))],
        out_shape=[jax.ShapeDtypeStruct((T, D), F32),
                   jax.ShapeDtypeStruct((T, D), BF16)],
        compiler_params=_params(("arbitrary",)),
        name="out_projection",
    )(attn, pool, x2d, mod3, norm_g, w_out)


def _oddeven_merge_sort_pairs(n):
    pairs = []

    def merge(lo, hi, r):
        step = r * 2
        if step < hi - lo:
            merge(lo, hi, step)
            merge(lo + r, hi, step)
            for k in range(lo + r, hi - r, step):
                pairs.append((k, k + r))
        else:
            pairs.append((lo, lo + r))

    def sort(lo, hi):
        if hi - lo >= 1:
            mid = lo + (hi - lo) // 2
            sort(lo, mid)
            sort(mid + 1, hi)
            merge(lo, hi, 1)

    sort(0, n - 1)
    return pairs


_SORT16 = _oddeven_merge_sort_pairs(TOPK)


def _sort_desc(vals):
    vals = list(vals)
    for a, b in _SORT16:
        hi, lo = jnp.maximum(vals[a], vals[b]), jnp.minimum(vals[a], vals[b])
        vals[a], vals[b] = hi, lo
    return vals


def _merge_top(a_list, b_list):
    n = TOPK
    c = [jnp.maximum(a_list[k], b_list[n - 1 - k]) for k in range(n)]
    stride = n // 2
    while stride >= 1:
        for k in range(n):
            if (k & stride) == 0:
                hi, lo = jnp.maximum(c[k], c[k + stride]), jnp.minimum(c[k], c[k + stride])
                c[k], c[k + stride] = hi, lo
        stride //= 2
    return c


def _top_across_sublanes(chunks, sub):
    half = SUBLANES // 2
    low = sub < half

    def pair(x, y):
        a = [jnp.where(low, xv, pltpu.roll(yv, half, axis=0)) for xv, yv in zip(x, y)]
        b = [jnp.where(low, pltpu.roll(xv, half, axis=0), yv) for xv, yv in zip(x, y)]
        return _merge_top(a, b)

    m01, m23 = pair(chunks[0], chunks[1]), pair(chunks[2], chunks[3])
    q0, q1, q2 = sub < 2, sub < 4, sub < 6
    a, b = [], []
    for u, v in zip(m01, m23):
        u6, u4 = pltpu.roll(u, 6, axis=0), pltpu.roll(u, 4, axis=0)
        v4, v2 = pltpu.roll(v, 4, axis=0), pltpu.roll(v, 2, axis=0)
        a.append(jnp.where(q1, jnp.where(q0, u, u6), jnp.where(q2, v4, v2)))
        b.append(jnp.where(q1, jnp.where(q0, u6, u4), jnp.where(q2, v2, v)))
    m = _merge_top(a, b)
    return _merge_top(m, [pltpu.roll(v, SUBLANES - 1, axis=0) for v in m])


_CAND_ROWS = [lst for d in range(TOPK) for lst in (
    [(d, j) for j in range(d, TOPK) if (d + 1) * (j + 1) <= TOPK],
    [(i, d) for i in range(d + 1, TOPK) if (i + 1) * (d + 1) <= TOPK]) if lst]


def _route_kernel(h_ref, wq_ref, keys_ref, e1_ref, e2_ref, thr_ref, qp_scr, top_scr,
                  *, tm, n_heads, n_keys):
    qp = jnp.dot(h_ref[...], wq_ref[...], preferred_element_type=F32)
    qp_scr[...] = qp.astype(BF16)
    n_chunks = tm // LANES
    groups = n_keys // SUBLANES
    assert groups == TOPK
    assert 2 * n_chunks == SUBLANES, "the four token chunks of a tile share the sublanes of a merge"
    sub = lax.broadcasted_iota(jnp.int32, (SUBLANES, LANES), 0)

    for h in range(n_heads):
        for half, out_ref in ((0, e1_ref), (1, e2_ref)):
            col0 = (2 * h + half) * n_keys
            s_t = lax.dot_general(keys_ref[h, half], qp_scr[:, col0:col0 + n_keys],
                                  (((1,), (1,)), ((), ())), preferred_element_type=F32)
            out_ref[h] = s_t
            chunks = []
            for cidx in range(n_chunks):
                blk = s_t[:, cidx * LANES:(cidx + 1) * LANES]
                chunks.append(_sort_desc([blk[g * SUBLANES:(g + 1) * SUBLANES, :] for g in range(groups)]))
            vals = _top_across_sublanes(chunks, sub)
            for k in range(TOPK):
                for cidx in range(n_chunks):
                    top_scr[half, k, h:h + 1, cidx * LANES:(cidx + 1) * LANES] = \
                        vals[k][2 * cidx:2 * cidx + 1, :]

    for cidx in range(n_chunks):
        lanes = slice(cidx * LANES, (cidx + 1) * LANES)
        a = [top_scr[0, k, :, lanes] for k in range(TOPK)]
        b = [top_scr[1, k, :, lanes] for k in range(TOPK)]
        ea = [jnp.exp(v - a[0]) for v in a]
        eb = [jnp.exp(v - b[0]) for v in b]
        pad = jnp.full_like(a[0], -1.0)
        best = None
        for row in _CAND_ROWS:
            lst = [ea[i] * eb[j] for (i, j) in row]
            lst = lst + [pad] * (TOPK - len(lst))
            best = lst if best is None else _merge_top(best, lst)
        z = best[0]
        for k in range(1, TOPK):
            z = z + best[k]
        rz = 1.0 / z
        cut = best[TOPK - 1]
        eas = [GATE_SCALE * v for v in ea]
        ebn = [v * rz for v in eb]
        thr = None
        for row in _CAND_ROWS:
            for (i, j) in row:
                sel = ea[i] * eb[j] >= cut
                cand = jnp.where(sel, eas[i] * ebn[j], jnp.inf)
                thr = cand if thr is None else jnp.minimum(thr, cand)
        thr_ref[:, lanes] = thr
        top_scr[0, 0, :, lanes] = a[0]
        top_scr[1, 0, :, lanes] = b[0]
        top_scr[1, 1, :, lanes] = rz

    for h in range(n_heads):
        m1 = top_scr[0, 0, h:h + 1, :]
        m2 = top_scr[1, 0, h:h + 1, :]
        rz = top_scr[1, 1, h:h + 1, :]
        e1_ref[h] = GATE_SCALE * jnp.exp(e1_ref[h] - m1)
        e2_ref[h] = jnp.exp(e2_ref[h] - m2) * rz


def _routing(h2, w_query, sub_keys, *, tm):
    T, D = h2.shape
    n_heads, _, n_keys, kd = sub_keys.shape
    qw = w_query.shape[1]
    kern = functools.partial(_route_kernel, tm=tm, n_heads=n_heads, n_keys=n_keys)
    key_blk = pl.BlockSpec((n_heads, n_keys, tm), lambda i: (0, 0, i))
    return pl.pallas_call(
        kern,
        grid=(T // tm,),
        in_specs=[pl.BlockSpec((tm, D), lambda i: (i, 0)),
                  pl.BlockSpec((D, qw), lambda i: (0, 0)),
                  pl.BlockSpec(sub_keys.shape, lambda i: (0, 0, 0, 0))],
        out_specs=[key_blk, key_blk, pl.BlockSpec((n_heads, tm), lambda i: (0, i))],
        out_shape=[jax.ShapeDtypeStruct((n_heads, n_keys, T), F32),
                   jax.ShapeDtypeStruct((n_heads, n_keys, T), F32),
                   jax.ShapeDtypeStruct((n_heads, T), F32)],
        scratch_shapes=[pltpu.VMEM((tm, qw), BF16),
                        pltpu.VMEM((2, TOPK, n_heads, tm), F32)],
        compiler_params=_params(("arbitrary",)),
        name="peer_routing",
    )(h2, w_query, sub_keys)


def _gelu_core(z):
    return z * (lax.erf(z) + 1.0)


def _expert_kernel(h_ref, e1_ref, e2_ref, thr_ref, u_ref, vt_ref, x1_ref, mod_ref, g_ref,
                   o_ref, acc_scr, act_scr, *, tm, te, n_heads, n_keys):
    j = pl.program_id(1)
    n_i1 = te // n_keys
    n_chunks = tm // LANES

    @pl.when(j == 0)
    def _():
        acc_scr[...] = jnp.zeros_like(acc_scr)

    a_t = lax.dot_general(u_ref[...], h_ref[...], (((1,), (1,)), ((), ())),
                          preferred_element_type=F32)
    for k in range(n_i1):
        for cidx in range(n_chunks):
            lanes = slice(cidx * LANES, (cidx + 1) * LANES)
            gate = None
            for h in range(n_heads):
                p = e1_ref[h, k:k + 1, lanes] * e2_ref[h, :, lanes]
                sel = jnp.where(p >= thr_ref[h:h + 1, lanes], p, 0.0)
                gate = sel if gate is None else gate + sel
            a_blk = a_t[k * n_keys:(k + 1) * n_keys, lanes]
            act_scr[k * n_keys:(k + 1) * n_keys, lanes] = (_gelu_core(a_blk) * gate).astype(BF16)

    for c in range(2):
        cols = slice(c * tm // 2, (c + 1) * tm // 2)
        acc_scr[:, cols] += jnp.dot(vt_ref[...], act_scr[:, cols], preferred_element_type=F32)

    @pl.when(j == pl.num_programs(1) - 1)
    def _():
        y = acc_scr[...].T
        x2 = x1_ref[...] + mod_ref[0, 5:6, :] * y
        ms = jnp.mean(x2 * x2, axis=-1, keepdims=True)
        o_ref[...] = x2 * lax.rsqrt(ms + NORM_EPS) * g_ref[...]


def _experts(h2, e1, e2, thr, peer_u, peer_vt, x1, mod3, final_g, *, seq, tm, te):
    T, D = h2.shape
    n_heads, n_keys, _ = e1.shape
    E = peer_u.shape[0]
    tiles_per_seq = seq // tm
    n_i1 = te // n_keys
    assert n_i1 == SUBLANES, "expert rows of one block fill the sublanes of an f32 tile"
    kern = functools.partial(_expert_kernel, tm=tm, te=te, n_heads=n_heads, n_keys=n_keys)
    row_blk = pl.BlockSpec((n_heads, n_i1, tm), lambda i, j: (0, j, i))
    key_blk = pl.BlockSpec((n_heads, n_keys, tm), lambda i, j: (0, 0, i))
    return pl.pallas_call(
        kern,
        grid=(T // tm, E // te),
        in_specs=[
            pl.BlockSpec((tm, D), lambda i, j: (i, 0)),
            row_blk, key_blk,
            pl.BlockSpec((n_heads, tm), lambda i, j: (0, i)),
            pl.BlockSpec((te, D), lambda i, j: (j, 0)),
            pl.BlockSpec((None, D, te), lambda i, j: (j, 0, 0)),
            pl.BlockSpec((tm, D), lambda i, j: (i, 0)),
            pl.BlockSpec((1, 6, D), lambda i, j: (i // tiles_per_seq, 0, 0)),
            pl.BlockSpec((1, D), lambda i, j: (0, 0)),
        ],
        out_specs=pl.BlockSpec((tm, D), lambda i, j: (i, 0)),
        out_shape=jax.ShapeDtypeStruct((T, D), F32),
        scratch_shapes=[pltpu.VMEM((D, tm), F32),
                        pltpu.VMEM((te, tm), BF16)],
        compiler_params=_params(("arbitrary", "arbitrary")),
        name="peer_experts",
    )(h2, e1, e2, thr, peer_u, peer_vt, x1, mod3, final_g)


def _value_blocks_kernel(v_ref, o_ref):
    o_ref[...] = v_ref[...].T.astype(o_ref.dtype)


def _value_blocks(peer_v, *, te, td):
    E, D = peer_v.shape
    return pl.pallas_call(
        _value_blocks_kernel,
        grid=(E // te, D // td),
        in_specs=[pl.BlockSpec((te, td), lambda j, d: (j, d))],
        out_specs=pl.BlockSpec((None, td, te), lambda j, d: (j, d, 0)),
        out_shape=jax.ShapeDtypeStruct((E // te, D, te), BF16),
        compiler_params=_params(("arbitrary", "arbitrary")),
        name="expert_value_blocks",
    )(peer_v)


def kernel(x, c, positions, w_mod, b_mod, norm1_g, w_in, w_pool, pool_scale, w_out, norm2_g,
           w_query, sub_keys, peer_u, peer_v, final_g):
    B, S, D = x.shape
    assert w_mod.shape[0] == 1, "single-layer stack"
    assert S % ATTN_SPAN == 0
    tm = min(512, S)
    aw = w_in.shape[2] // 4
    x2d = x.reshape(B * S, D)

    mod3 = _modulation(c, w_mod[0], b_mod[0]).reshape(B, 6, D)
    qkv, pool = _in_projection(x2d, mod3, norm1_g[0].reshape(1, D), _rope_tables(positions),
                               w_in[0].astype(BF16), w_pool[0].astype(BF16),
                               pool_scale[0].reshape(1, -1), seq=S, tm=tm)
    attn = _attention(qkv, batch=B, seq=S, attn_width=aw)
    x1, h2 = _out_projection(attn, pool, x2d, mod3, norm2_g[0].reshape(1, D),
                             w_out[0].astype(BF16), seq=S, tm=tm)
    e1, e2, thr = _routing(h2, w_query[0].astype(BF16), sub_keys[0].astype(BF16), tm=tm)
    te = 1024
    vt_blocks = _value_blocks(peer_v[0], te=te, td=D)
    u_scaled = (peer_u[0] * PRE_SCALE).astype(BF16)
    out = _experts(h2, e1, e2, thr, u_scaled, vt_blocks,
                   x1, mod3, final_g.reshape(1, D), seq=S, tm=tm, te=te)
    return out.reshape(B, S, D)
```
